```python
import math
import jax, jax.numpy as jnp
from jax import lax
import numpy as np

D_MODEL = 2048
BATCH = 4
SEQ = 2048
DEPTH = 1

MIX_WIDTH = D_MODEL
SSM_WIDTH = MIX_WIDTH // 2
ATTN_WIDTH = MIX_WIDTH - SSM_WIDTH
SSM_GROUP = 16
SSM_GROUPS = SSM_WIDTH // SSM_GROUP
SSM_STATE = 64
DT_MIN = 0.001
DT_MAX = 0.1
HEAD_DIM = 128
N_HEADS = ATTN_WIDTH // HEAD_DIM
DILATION_PAIRS = ((128, 1), (512, 4), (2048, 16))
Q_BLOCK = 128
ROPE_THETA = 10000.0
D_FF = -(-8 * D_MODEL // (3 * 256)) * 256
PROJ_WIDTH = SSM_WIDTH + 3 * ATTN_WIDTH
N_MOD = 6
EPS = 1e-6
NEG = -1e30

kernel_name = "hymba_s5_dilated_attn_block"


def rmsnorm(x, g):
    xf = x.astype(jnp.float32)
    y = xf * lax.rsqrt(jnp.mean(xf * xf, axis=-1, keepdims=True) + EPS)
    return (y * g.astype(jnp.float32)).astype(x.dtype)


def rope(t, pos):
    half = t.shape[-1] // 2
    inv_freq = ROPE_THETA ** (-jnp.arange(half, dtype=jnp.float32) / half)
    ang = pos.astype(jnp.float32)[:, :, None] * inv_freq
    cos = jnp.cos(ang)[:, :, None, :]
    sin = jnp.sin(ang)[:, :, None, :]
    t1, t2 = t[..., :half], t[..., half:]
    return jnp.concatenate([t1 * cos - t2 * sin, t1 * sin + t2 * cos], axis=-1)


def s5_mixer(u, a_re, a_im, log_dt, b_re, b_im, c_re, c_im, d_skip, w_glu, b_glu):
    bsz, s, _ = u.shape
    f32 = jnp.float32
    uf = u.astype(f32).reshape(bsz, s, SSM_GROUPS, SSM_GROUP)
    a_re = a_re.astype(f32); a_im = a_im.astype(f32)
    b_re = b_re.astype(f32); b_im = b_im.astype(f32)
    c_re = c_re.astype(f32); c_im = c_im.astype(f32)
    dt = jnp.exp(log_dt.astype(f32))[:, None]
    mag = jnp.exp(a_re * dt)
    abar_re = mag * jnp.cos(a_im * dt)
    abar_im = mag * jnp.sin(a_im * dt)
    den = a_re * a_re + a_im * a_im
    nr = abar_re - 1.0
    ni = abar_im
    f_re = (nr * a_re + ni * a_im) / den
    f_im = (ni * a_re - nr * a_im) / den
    bb_re = f_re[..., None] * b_re - f_im[..., None] * b_im
    bb_im = f_re[..., None] * b_im + f_im[..., None] * b_re
    bu_re = jnp.einsum('bsgc,gpc->bsgp', uf, bb_re)
    bu_im = jnp.einsum('bsgc,gpc->bsgp', uf, bb_im)
    ar_full = jnp.broadcast_to(abar_re, bu_re.shape)
    ai_full = jnp.broadcast_to(abar_im, bu_re.shape)

    def combine(e1, e2):
        a1r, a1i, b1r, b1i = e1
        a2r, a2i, b2r, b2i = e2
        return (a2r * a1r - a2i * a1i,
                a2r * a1i + a2i * a1r,
                a2r * b1r - a2i * b1i + b2r,
                a2r * b1i + a2i * b1r + b2i)

    _, _, st_re, st_im = lax.associative_scan(combine, (ar_full, ai_full, bu_re, bu_im), axis=1)
    y = (jnp.einsum('bsgp,gcp->bsgc', st_re, c_re)
         - jnp.einsum('bsgp,gcp->bsgc', st_im, c_im))
    y = y + d_skip.astype(f32).reshape(SSM_GROUPS, SSM_GROUP) * uf
    y = y.reshape(bsz, s, SSM_WIDTH)
    vg = jax.nn.gelu(y)
    out = vg * jax.nn.sigmoid(vg @ w_glu.astype(f32) + b_glu.astype(f32))
    return out.astype(u.dtype)


def dilated_branch(q, k, v, dilation, steps):
    bsz, s, h, e = q.shape
    seg = s // dilation
    nb = -(-seg // Q_BLOCK)
    segp = nb * Q_BLOCK

    def to_blocks(t):
        t = t.reshape(bsz, seg, dilation, h, e)
        t = jnp.pad(t, ((0, 0), (0, segp - seg), (0, 0), (0, 0), (0, 0)))
        return t.reshape(bsz, nb, Q_BLOCK, dilation, h, e)

    def with_prev(t):
        prev = jnp.pad(t[:, :-1], ((0, 0), (1, 0), (0, 0), (0, 0), (0, 0), (0, 0)))
        return jnp.concatenate([prev, t], axis=2)

    qb = to_blocks(q)
    kc = with_prev(to_blocks(k))
    vc = with_prev(to_blocks(v))
    sc = jnp.einsum('bnqrhe,bnkrhe->bnrhqk', qb, kc) * (e ** -0.5)
    qi = jnp.arange(Q_BLOCK)[:, None] + Q_BLOCK
    ki = jnp.arange(2 * Q_BLOCK)[None, :]
    dist = qi - ki
    band = (dist >= 0) & (dist <= steps)
    kglob = jnp.arange(nb)[:, None] * Q_BLOCK + ki - Q_BLOCK
    mask = band[None] & (kglob >= 0)[:, None, :]
    sc = jnp.where(mask[None, :, None, None], sc, NEG)
    m = jnp.max(sc, axis=-1, keepdims=True)
    p = jnp.exp(sc - m)
    l = jnp.sum(p, axis=-1, keepdims=True)
    o = jnp.einsum('bnrhqk,bnkrhe->bnrhqe', p, vc) / l
    lse = (m + jnp.log(l))[..., 0]
    o = o.transpose(0, 1, 4, 2, 3, 5).reshape(bsz, segp, dilation, h, e)[:, :seg]
    lse = lse.transpose(0, 1, 4, 2, 3).reshape(bsz, segp, dilation, h)[:, :seg]
    return o.reshape(bsz, s, h, e), lse.reshape(bsz, s, h)


def dilated_attention(q, k, v, positions):
    bsz, s, _ = q.shape
    f32 = jnp.float32
    q = rope(q.astype(f32).reshape(bsz, s, N_HEADS, HEAD_DIM), positions)
    k = rope(k.astype(f32).reshape(bsz, s, N_HEADS, HEAD_DIM), positions)
    vh = v.astype(f32).reshape(bsz, s, N_HEADS, HEAD_DIM)
    outs = []
    lses = []
    for window, dilation in DILATION_PAIRS:
        o, lse = dilated_branch(q, k, vh, dilation, window // dilation)
        outs.append(o)
        lses.append(lse)
    wts = jax.nn.softmax(jnp.stack(lses, axis=0), axis=0)
    o = jnp.sum(wts[..., None] * jnp.stack(outs, axis=0), axis=0)
    return o.reshape(bsz, s, ATTN_WIDTH).astype(v.dtype)


def setup_inputs(seed: int = 0) -> dict:
    key = jax.random.key(seed)
    ks = jax.random.split(key, 32)
    f32 = jnp.float32
    nrm = lambda k, shape, scale: jax.random.normal(k, shape, f32) * scale
    x = nrm(ks[0], (BATCH, SEQ, D_MODEL), 1.0)
    c = nrm(ks[1], (BATCH, D_MODEL), 1.0)
    offs = jax.random.randint(ks[2], (BATCH, 1), 0, 4096, dtype=jnp.int32)
    positions = offs + jnp.arange(SEQ, dtype=jnp.int32)[None, :]
    w_mod = nrm(ks[3], (DEPTH, D_MODEL, N_MOD * D_MODEL), 0.5 * D_MODEL ** -0.5)
    b_mod = nrm(ks[4], (DEPTH, N_MOD * D_MODEL), 0.02)
    g_mix = 1.0 + nrm(ks[5], (DEPTH, D_MODEL), 0.01)
    w_in = nrm(ks[6], (DEPTH, D_MODEL, PROJ_WIDTH), D_MODEL ** -0.5)
    n_idx = jnp.arange(SSM_STATE, dtype=f32)
    ssm_a_re = -0.5 + nrm(ks[7], (DEPTH, SSM_GROUPS, SSM_STATE), 0.01)
    ssm_a_im = math.pi * n_idx + nrm(ks[8], (DEPTH, SSM_GROUPS, SSM_STATE), 0.01)
    ssm_log_dt = jax.random.uniform(ks[9], (DEPTH, SSM_GROUPS), f32,
                                    math.log(DT_MIN), math.log(DT_MAX))
    ssm_b_re = nrm(ks[10], (DEPTH, SSM_GROUPS, SSM_STATE, SSM_GROUP), (2 * SSM_GROUP) ** -0.5)
    ssm_b_im = nrm(ks[11], (DEPTH, SSM_GROUPS, SSM_STATE, SSM_GROUP), (2 * SSM_GROUP) ** -0.5)
    ssm_c_re = nrm(ks[12], (DEPTH, SSM_GROUPS, SSM_GROUP, SSM_STATE), 1.0)
    ssm_c_im = nrm(ks[13], (DEPTH, SSM_GROUPS, SSM_GROUP, SSM_STATE), 1.0)
    ssm_d = nrm(ks[14], (DEPTH, SSM_WIDTH), 0.5)
    w_glu = nrm(ks[15], (DEPTH, SSM_WIDTH, SSM_WIDTH), SSM_WIDTH ** -0.5)
    b_glu = nrm(ks[16], (DEPTH, SSM_WIDTH), 0.01)
    g_ssm_out = 1.0 + nrm(ks[17], (DEPTH, SSM_WIDTH), 0.01)
    g_attn_out = 1.0 + nrm(ks[18], (DEPTH, ATTN_WIDTH), 0.01)
    w_out = nrm(ks[19], (DEPTH, MIX_WIDTH, D_MODEL), MIX_WIDTH ** -0.5)
    g_ffn = 1.0 + nrm(ks[20], (DEPTH, D_MODEL), 0.01)
    w_gate = nrm(ks[21], (DEPTH, D_MODEL, D_FF), D_MODEL ** -0.5)
    w_up = nrm(ks[22], (DEPTH, D_MODEL, D_FF), D_MODEL ** -0.5)
    w_down = nrm(ks[23], (DEPTH, D_FF, D_MODEL), D_FF ** -0.5)
    g_final = 1.0 + nrm(ks[24], (D_MODEL,), 0.01)
    return {"x": x, "c": c, "positions": positions, "w_mod": w_mod, "b_mod": b_mod,
            "g_mix": g_mix, "w_in": w_in, "ssm_a_re": ssm_a_re, "ssm_a_im": ssm_a_im,
            "ssm_log_dt": ssm_log_dt, "ssm_b_re": ssm_b_re, "ssm_b_im": ssm_b_im,
            "ssm_c_re": ssm_c_re, "ssm_c_im": ssm_c_im, "ssm_d": ssm_d, "w_glu": w_glu,
            "b_glu": b_glu, "g_ssm_out": g_ssm_out, "g_attn_out": g_attn_out, "w_out": w_out,
            "g_ffn": g_ffn, "w_gate": w_gate, "w_up": w_up, "w_down": w_down,
            "g_final": g_final}


def reference(x, c, positions, w_mod, b_mod, g_mix, w_in, ssm_a_re, ssm_a_im, ssm_log_dt,
              ssm_b_re, ssm_b_im, ssm_c_re, ssm_c_im, ssm_d, w_glu, b_glu, g_ssm_out,
              g_attn_out, w_out, g_ffn, w_gate, w_up, w_down, g_final):
    for l in range(DEPTH):
        mod = (jax.nn.silu(c) @ w_mod[l] + b_mod[l])[:, None, :]
        sh1, sc1, gt1, sh2, sc2, gt2 = jnp.split(mod, N_MOD, axis=-1)
        h = rmsnorm(x, g_mix[l]) * (1.0 + sc1) + sh1
        proj = h @ w_in[l]
        u, q, k, v = jnp.split(proj, [SSM_WIDTH, SSM_WIDTH + ATTN_WIDTH,
                                      SSM_WIDTH + 2 * ATTN_WIDTH], axis=-1)
        y_ssm = s5_mixer(u, ssm_a_re[l], ssm_a_im[l], ssm_log_dt[l], ssm_b_re[l], ssm_b_im[l],
                         ssm_c_re[l], ssm_c_im[l], ssm_d[l], w_glu[l], b_glu[l])
        y_att = dilated_attention(q, k, v, positions)
        y = jnp.concatenate([rmsnorm(y_ssm, g_ssm_out[l]), rmsnorm(y_att, g_attn_out[l])], axis=-1)
        x = x + gt1 * (y @ w_out[l])
        h = rmsnorm(x, g_ffn[l]) * (1.0 + sc2) + sh2
        f = (jax.nn.silu(h @ w_gate[l]) * (h @ w_up[l])) @ w_down[l]
        x = x + gt2 * f
    return rmsnorm(x, g_final)
```

```python
import functools

import jax
import jax.numpy as jnp
from jax import lax
from jax.experimental import pallas as pl
from jax.experimental.pallas import tpu as pltpu

F32 = jnp.float32
BF16 = jnp.bfloat16

SSM_GROUP = 16
SSM_STATE = 64
HEAD_DIM = 128
DILATION_PAIRS = ((128, 1), (512, 4), (2048, 16))
Q_BLOCK = 128
ROPE_THETA = 10000.0
N_MOD = 6
EPS = 1e-6
NEG = -1e30

LANES = 128
SUBLANES = 8
VMEM_LIMIT = 52 * 1024 * 1024

MOD_TN = 1024
PROJ_TM = 512
PROJ_TN = 1024
FFN_TF = 512
SSM_TC = 256
PAIRS = 32
PITCH = PAIRS + 4
PAIR_VREGS = PAIRS // SUBLANES


def _params(sem):
    return pltpu.CompilerParams(dimension_semantics=sem, vmem_limit_bytes=VMEM_LIMIT)


def _const_spec(shape):
    nd = len(shape)
    return pl.BlockSpec(shape, lambda *_: (0,) * nd, pipeline_mode=pl.Buffered(1))


def _mod_kernel(c_ref, w_ref, b_ref, o_ref):
    c = c_ref[...]
    s = (c * jax.nn.sigmoid(c)).astype(BF16)
    o_ref[...] = jnp.dot(s, w_ref[...].astype(BF16), preferred_element_type=F32) + b_ref[...]


def _modulation(c_pad, w_mod, b_mod):
    rows, d = c_pad.shape
    n = w_mod.shape[1]
    return pl.pallas_call(
        _mod_kernel,
        grid=(n // MOD_TN,),
        in_specs=[
            pl.BlockSpec((rows, d), lambda j: (0, 0)),
            pl.BlockSpec((d, MOD_TN), lambda j: (0, j)),
            pl.BlockSpec((1, MOD_TN), lambda j: (0, j)),
        ],
        out_specs=pl.BlockSpec((rows, MOD_TN), lambda j: (0, j)),
        out_shape=jax.ShapeDtypeStruct((rows, n), F32),
        compiler_params=_params(("parallel",)),
        name="mod",
    )(c_pad, w_mod, b_mod.reshape(1, n))


def _modulated_norm(x, g, sc, sh):
    ms = jnp.mean(x * x, axis=-1, keepdims=True)
    return x * lax.rsqrt(ms + EPS) * (g * (1.0 + sc)) + sh


def _proj_kernel(x_ref, g_ref, sc_ref, sh_ref, w_ref, o_ref, h_ref):
    h_ref[...] = _modulated_norm(x_ref[0], g_ref[...], sc_ref[0], sh_ref[0]).astype(BF16)
    for n in range(w_ref.shape[1] // PROJ_TN):
        cols = slice(n * PROJ_TN, (n + 1) * PROJ_TN)
        o_ref[0, :, cols] = jnp.dot(h_ref[...], w_ref[:, cols],
                                    preferred_element_type=F32).astype(BF16)


def _projection(x, g, sc, sh, w_in):
    b, s, d = x.shape
    n = w_in.shape[1]
    row = lambda bi, i: (bi, i, 0)
    per_batch = lambda bi, i: (bi, 0, 0)
    return pl.pallas_call(
        _proj_kernel,
        grid=(b, s // PROJ_TM),
        in_specs=[
            pl.BlockSpec((1, PROJ_TM, d), row),
            _const_spec((1, d)),
            pl.BlockSpec((1, 1, d), per_batch),
            pl.BlockSpec((1, 1, d), per_batch),
            _const_spec((d, n)),
        ],
        out_specs=pl.BlockSpec((1, PROJ_TM, n), row),
        out_shape=jax.ShapeDtypeStruct((b, s, n), BF16),
        scratch_shapes=[pltpu.VMEM((PROJ_TM, d), BF16)],
        compiler_params=_params(("parallel", "parallel")),
        name="proj",
    )(x, g.reshape(1, d), sc, sh, w_in)


def _ssm_tables(a_re, a_im, log_dt, b_re, b_im, c_re, c_im):
    a_re = a_re.astype(F32); a_im = a_im.astype(F32)
    b_re = b_re.astype(F32); b_im = b_im.astype(F32)
    c_re = c_re.astype(F32); c_im = c_im.astype(F32)
    dt = jnp.exp(log_dt.astype(F32))[:, None]
    mag = jnp.exp(a_re * dt)
    abar_re = mag * jnp.cos(a_im * dt)
    abar_im = mag * jnp.sin(a_im * dt)
    den = a_re * a_re + a_im * a_im
    nr = abar_re - 1.0
    ni = abar_im
    f_re = (nr * a_re + ni * a_im) / den
    f_im = (ni * a_re - nr * a_im) / den
    bb_re = f_re[..., None] * b_re - f_im[..., None] * b_im
    bb_im = f_re[..., None] * b_im + f_im[..., None] * b_re

    g, p = a_re.shape
    cg = b_re.shape[-1]
    pairs = g // 2
    eye2 = jnp.eye(2, dtype=F32)

    def pair_in(t):
        t = t.transpose(0, 2, 1).reshape(pairs, 2, cg, p)
        return jnp.einsum("rkcp,kj->rkcjp", t, eye2).reshape(pairs, 2 * cg, 2 * p)

    blk = jnp.concatenate([pair_in(bb_re), pair_in(bb_im)], axis=-1)
    per_tile = LANES // (2 * cg)
    sel = jax.nn.one_hot(jnp.arange(pairs) % per_tile, per_tile, dtype=F32)
    bb = jnp.einsum("rcn,rq->rqcn", blk, sel).reshape(pairs, LANES, 4 * p)

    def pair_out(t):
        t = t.transpose(0, 2, 1).reshape(pairs, 2, p, cg)
        return jnp.einsum("rkpc,kj->rkpjc", t, eye2).reshape(pairs, 2 * p, 2 * cg)

    cblk = jnp.concatenate([pair_out(c_re), pair_out(-c_im)], axis=1)
    per_out = (2 * LANES) // (2 * cg)
    osel = jax.nn.one_hot(jnp.arange(pairs) % per_out, per_out, dtype=F32)
    cc = jnp.einsum("rkc,rq->rkqc", cblk, osel).reshape(pairs, 4 * p, 2 * LANES)

    def scan_rows(t):
        return t.reshape(SUBLANES, PAIR_VREGS, LANES).transpose(1, 0, 2)

    return bb.astype(BF16), cc.astype(BF16), scan_rows(abar_re), scan_rows(abar_im)


def _ssm_kernel(u_ref, bb_ref, ar_ref, ai_ref, cc_ref, d_ref, wglu_ref, bglu_ref, g_ref,
                o_ref, zre, zim, st, y_s):
    tc = u_ref.shape[1]

    @pl.when(pl.program_id(1) == 0)
    def _():
        st[...] = jnp.zeros_like(st)

    pairs_per_tile = PAIRS // (u_ref.shape[2] // LANES)
    for r in range(PAIRS):
        j = r // pairs_per_tile
        bu = jnp.dot(u_ref[0, :, j * LANES:(j + 1) * LANES], bb_ref[r],
                     preferred_element_type=F32)
        zre[pl.ds(r, tc, stride=PITCH), :] = bu[:, :LANES]
        zim[pl.ds(r, tc, stride=PITCH), :] = bu[:, LANES:]

    ar = [ar_ref[k] for k in range(PAIR_VREGS)]
    ai = [ai_ref[k] for k in range(PAIR_VREGS)]

    def step(t, carry):
        xr, xi = carry
        base = t * PITCH
        nxr, nxi = [], []
        for k in range(PAIR_VREGS):
            rows = pl.ds(base + k, SUBLANES, stride=PAIR_VREGS)
            nr = ar[k] * xr[k] - ai[k] * xi[k] + zre[rows, :]
            ni = ar[k] * xi[k] + ai[k] * xr[k] + zim[rows, :]
            zre[rows, :] = nr
            zim[rows, :] = ni
            nxr.append(nr)
            nxi.append(ni)
        return tuple(nxr), tuple(nxi)

    carry0 = (tuple(st[0, k] for k in range(PAIR_VREGS)),
              tuple(st[1, k] for k in range(PAIR_VREGS)))
    xr, xi = lax.fori_loop(0, tc, step, carry0, unroll=4)
    for k in range(PAIR_VREGS):
        st[0, k] = xr[k]
        st[1, k] = xi[k]

    out_tile = 2 * LANES
    pairs_per_out = PAIRS // (o_ref.shape[2] // out_tile)
    for j in range(o_ref.shape[2] // out_tile):
        acc = None
        for q in range(pairs_per_out):
            r = j * pairs_per_out + q
            lhs = jnp.concatenate([zre[pl.ds(r, tc, stride=PITCH), :].astype(BF16),
                                   zim[pl.ds(r, tc, stride=PITCH), :].astype(BF16)], axis=1)
            part = jnp.dot(lhs, cc_ref[r], preferred_element_type=F32)
            acc = part if acc is None else acc + part
        cols = slice(j * out_tile, (j + 1) * out_tile)
        y = acc + d_ref[:, cols] * u_ref[0, :, cols].astype(F32)
        y_s[:, cols] = jax.nn.gelu(y)

    vg = y_s[...]
    z = jnp.dot(vg.astype(BF16), wglu_ref[...], preferred_element_type=F32) + bglu_ref[...]
    out = vg * jax.nn.sigmoid(z)
    ms = jnp.mean(out * out, axis=-1, keepdims=True)
    o_ref[0] = (out * lax.rsqrt(ms + EPS) * g_ref[...]).astype(BF16)


def _ssm(proj, tables, d_skip, w_glu, b_glu, g_out):
    b, s, _ = proj.shape
    bb, cc, ar, ai = tables
    w = w_glu.shape[0]
    assert PAIRS * 2 * SSM_GROUP == w
    return pl.pallas_call(
        _ssm_kernel,
        grid=(b, s // SSM_TC),
        in_specs=[
            pl.BlockSpec((1, SSM_TC, w), lambda bi, t: (bi, t, 0)),
            _const_spec(bb.shape),
            _const_spec(ar.shape),
            _const_spec(ai.shape),
            _const_spec(cc.shape),
            _const_spec((1, w)),
            _const_spec((w, w)),
            _const_spec((1, w)),
            _const_spec((1, w)),
        ],
        out_specs=pl.BlockSpec((1, SSM_TC, w), lambda bi, t: (bi, t, 0)),
        out_shape=jax.ShapeDtypeStruct((b, s, w), BF16),
        scratch_shapes=[
            pltpu.VMEM((SSM_TC * PITCH, LANES), F32),
            pltpu.VMEM((SSM_TC * PITCH, LANES), F32),
            pltpu.VMEM((2, PAIR_VREGS, SUBLANES, LANES), F32),
            pltpu.VMEM((SSM_TC, w), F32),
        ],
        compiler_params=_params(("parallel", "arbitrary")),
        name="ssm",
    )(proj, bb, ar, ai, cc, d_skip.reshape(1, w), w_glu, b_glu.reshape(1, w),
      g_out.reshape(1, w))


def _attn_kernel(pos_ref, invf_ref, q_ref, k_ref, v_ref, g_ref, o_ref,
                 cos_s, sin_s, q_s, k_s, v_s, o_s, l_s, y_s):
    s = q_ref.shape[1]
    n_heads = y_s.shape[0]
    h = pl.program_id(1)
    half = HEAD_DIM // 2

    @pl.when(h == 0)
    def _():
        ang = pos_ref[0].astype(F32) * invf_ref[...]
        lane = lax.broadcasted_iota(jnp.int32, (1, HEAD_DIM), 1)
        cos_s[...] = jnp.cos(ang)
        sin_s[...] = jnp.sin(ang) * jnp.where(lane < half, -1.0, 1.0)

    def rope(t):
        return t * cos_s[...] + pltpu.roll(t, half, 1) * sin_s[...]

    q_s[...] = rope(q_ref[0].astype(F32)) * (HEAD_DIM ** -0.5)
    k_s[...] = rope(k_ref[0].astype(F32))
    v_s[...] = v_ref[0].astype(F32)


    for bi, (window, d) in enumerate(DILATION_PAIRS):
        steps = window // d
        span = Q_BLOCK * d
        nb = s // span

        def rows(start, d=d):
            if d == 1:
                return pl.ds(pl.multiple_of(start, Q_BLOCK), Q_BLOCK)
            return pl.ds(start, Q_BLOCK, stride=d)

        def block(i, carry, bi=bi, d=d, steps=steps, span=span, nb=nb, rows=rows):
            r = i % d
            n = i // d
            start = r + n * span
            cur = rows(start)
            q = q_s[cur, :].astype(BF16)
            n_keys = Q_BLOCK if nb == 1 else 2 * Q_BLOCK
            row = lax.broadcasted_iota(jnp.int32, (Q_BLOCK, n_keys), 0)
            col = lax.broadcasted_iota(jnp.int32, (Q_BLOCK, n_keys), 1)
            if nb == 1:
                kk = k_s[cur, :]
                vv = v_s[cur, :]
                dist = row - col
                mask = (dist >= 0) & (dist <= steps)
            else:
                prev = rows(jnp.maximum(start - span, r))
                kk = jnp.concatenate([k_s[prev, :], k_s[cur, :]], axis=0)
                vv = jnp.concatenate([v_s[prev, :], v_s[cur, :]], axis=0)
                dist = row + Q_BLOCK - col
                first_key = jnp.where(n > 0, 0, Q_BLOCK)
                mask = (dist >= 0) & (dist <= steps) & (col >= first_key)
            sc = lax.dot_general(q, kk.astype(BF16), (((1,), (1,)), ((), ())),
                                 preferred_element_type=F32)
            sc = jnp.where(mask, sc, NEG)
            m = jnp.max(sc, axis=-1, keepdims=True)
            p = jnp.exp(sc - m)
            l = jnp.sum(p, axis=-1, keepdims=True)
            pv = jnp.dot(p.astype(BF16), vv.astype(BF16), preferred_element_type=F32)
            o_s[bi, cur, :] = pv / l
            l_s[bi, cur, :] = jnp.broadcast_to(m + jnp.log(l), (Q_BLOCK, HEAD_DIM))
            return carry

        lax.fori_loop(0, s // Q_BLOCK, block, 0)

    lses = [l_s[bi] for bi in range(len(DILATION_PAIRS))]
    top = functools.reduce(jnp.maximum, lses)
    es = [jnp.exp(x - top) for x in lses]
    num = sum(e * o_s[bi] for bi, e in enumerate(es))
    y_s[h] = num / sum(es)

    @pl.when(h == n_heads - 1)
    def _():
        ss = sum(jnp.sum(y_s[i] * y_s[i], axis=-1, keepdims=True) for i in range(n_heads))
        inv = lax.rsqrt(ss / (n_heads * HEAD_DIM) + EPS)
        for i in range(n_heads):
            cols = slice(i * HEAD_DIM, (i + 1) * HEAD_DIM)
            o_ref[0, :, cols] = (y_s[i] * inv * g_ref[:, cols]).astype(BF16)


def _attention(proj, positions, g_out, ssm_width):
    b, s, _ = proj.shape
    w = g_out.shape[0]
    n_heads = w // HEAD_DIM
    half = HEAD_DIM // 2
    inv_freq = ROPE_THETA ** (-jnp.arange(half, dtype=F32) / half)
    inv_freq = jnp.concatenate([inv_freq, inv_freq]).reshape(1, HEAD_DIM)
    first = ssm_width // HEAD_DIM

    def head_cols(offset):
        return pl.BlockSpec((1, s, HEAD_DIM), lambda bi, h: (bi, 0, first + offset * n_heads + h))

    n_br = len(DILATION_PAIRS)
    return pl.pallas_call(
        _attn_kernel,
        grid=(b, n_heads),
        in_specs=[
            pl.BlockSpec((1, s, 1), lambda bi, h: (bi, 0, 0)),
            _const_spec((1, HEAD_DIM)),
            head_cols(0), head_cols(1), head_cols(2),
            _const_spec((1, w)),
        ],
        out_specs=pl.BlockSpec((1, s, w), lambda bi, h: (bi, 0, 0)),
        out_shape=jax.ShapeDtypeStruct((b, s, w), BF16),
        scratch_shapes=[
            pltpu.VMEM((s, HEAD_DIM), F32),
            pltpu.VMEM((s, HEAD_DIM), F32),
            pltpu.VMEM((s, HEAD_DIM), F32),
            pltpu.VMEM((s, HEAD_DIM), F32),
            pltpu.VMEM((s, HEAD_DIM), F32),
            pltpu.VMEM((n_br, s, HEAD_DIM), F32),
            pltpu.VMEM((n_br, s, HEAD_DIM), F32),
            pltpu.VMEM((n_heads, s, HEAD_DIM), F32),
        ],
        compiler_params=_params(("parallel", "arbitrary")),
        name="attn",
    )(positions.reshape(b, s, 1), inv_freq, proj, proj, proj, g_out.reshape(1, w))


def _outproj_kernel(x_ref, ys_ref, ya_ref, gt_ref, w_ref, o_ref):
    ws = ys_ref.shape[2]
    acc = jnp.dot(ys_ref[0], w_ref[:ws, :], preferred_element_type=F32)
    acc = acc + jnp.dot(ya_ref[0], w_ref[ws:, :], preferred_element_type=F32)
    o_ref[0] = x_ref[0] + gt_ref[0] * acc


def _out_projection(x, y_ssm, y_att, gt, w_out):
    b, s, d = x.shape
    row = lambda bi, i: (bi, i, 0)
    return pl.pallas_call(
        _outproj_kernel,
        grid=(b, s // PROJ_TM),
        in_specs=[
            pl.BlockSpec((1, PROJ_TM, d), row),
            pl.BlockSpec((1, PROJ_TM, y_ssm.shape[2]), row),
            pl.BlockSpec((1, PROJ_TM, y_att.shape[2]), row),
            pl.BlockSpec((1, 1, d), lambda bi, i: (bi, 0, 0)),
            _const_spec(w_out.shape),
        ],
        out_specs=pl.BlockSpec((1, PROJ_TM, d), row),
        out_shape=jax.ShapeDtypeStruct((b, s, d), F32),
        compiler_params=_params(("parallel", "parallel")),
        name="outproj",
    )(x, y_ssm, y_att, gt, w_out)


def _ffn_kernel(x_ref, g_ref, sc_ref, sh_ref, gt_ref, gf_ref, wg_ref, wu_ref, wd_ref,
                o_ref, h_ref, acc_ref, *, final_norm):
    j = pl.program_id(2)

    @pl.when(j == 0)
    def _():
        h_ref[...] = _modulated_norm(x_ref[0], g_ref[...], sc_ref[0], sh_ref[0]).astype(BF16)
        acc_ref[...] = jnp.zeros_like(acc_ref)

    h = h_ref[...]
    gate = jnp.dot(h, wg_ref[...], preferred_element_type=F32)
    up = jnp.dot(h, wu_ref[...], preferred_element_type=F32)
    act = (gate * jax.nn.sigmoid(gate) * up).astype(BF16)
    acc_ref[...] += jnp.dot(act, wd_ref[...], preferred_element_type=F32)

    @pl.when(j == pl.num_programs(2) - 1)
    def _():
        y = x_ref[0] + gt_ref[0] * acc_ref[...]
        if final_norm:
            ms = jnp.mean(y * y, axis=-1, keepdims=True)
            y = y * lax.rsqrt(ms + EPS) * gf_ref[...]
        o_ref[0] = y


def _ffn(x, g, sc, sh, gt, g_final, w_gate, w_up, w_down, final_norm):
    b, s, d = x.shape
    dff = w_gate.shape[1]
    row = lambda bi, i, j: (bi, i, 0)
    per_batch = lambda bi, i, j: (bi, 0, 0)
    return pl.pallas_call(
        functools.partial(_ffn_kernel, final_norm=final_norm),
        grid=(b, s // PROJ_TM, dff // FFN_TF),
        in_specs=[
            pl.BlockSpec((1, PROJ_TM, d), row),
            _const_spec((1, d)),
            pl.BlockSpec((1, 1, d), per_batch),
            pl.BlockSpec((1, 1, d), per_batch),
            pl.BlockSpec((1, 1, d), per_batch),
            _const_spec((1, d)),
            pl.BlockSpec((d, FFN_TF), lambda bi, i, j: (0, j)),
            pl.BlockSpec((d, FFN_TF), lambda bi, i, j: (0, j)),
            pl.BlockSpec((FFN_TF, d), lambda bi, i, j: (j, 0)),
        ],
        out_specs=pl.BlockSpec((1, PROJ_TM, d), row),
        out_shape=jax.ShapeDtypeStruct((b, s, d), F32),
        scratch_shapes=[pltpu.VMEM((PROJ_TM, d), BF16), pltpu.VMEM((PROJ_TM, d), F32)],
        compiler_params=_params(("parallel", "parallel", "arbitrary")),
        name="ffn",
    )(x, g.reshape(1, d), sc, sh, gt, g_final.reshape(1, d), w_gate, w_up, w_down)


def kernel(x, c, positions, w_mod, b_mod, g_mix, w_in, ssm_a_re, ssm_a_im, ssm_log_dt,
           ssm_b_re, ssm_b_im, ssm_c_re, ssm_c_im, ssm_d, w_glu, b_glu, g_ssm_out,
           g_attn_out, w_out, g_ffn, w_gate, w_up, w_down, g_final):
    depth = w_mod.shape[0]
    b, s, d = x.shape
    ssm_width = w_glu.shape[1]
    c_pad = jnp.zeros((SUBLANES, d), F32).at[:b].set(c.astype(F32))
    x = x.astype(F32)
    for l in range(depth):
        mod = _modulation(c_pad, w_mod[l], b_mod[l])[:b]
        sh1, sc1, gt1, sh2, sc2, gt2 = [m.reshape(b, 1, d) for m in jnp.split(mod, N_MOD, axis=-1)]
        proj = _projection(x, g_mix[l], sc1, sh1, w_in[l].astype(BF16))
        tables = _ssm_tables(ssm_a_re[l], ssm_a_im[l], ssm_log_dt[l], ssm_b_re[l], ssm_b_im[l],
                             ssm_c_re[l], ssm_c_im[l])
        y_ssm = _ssm(proj, tables, ssm_d[l], w_glu[l].astype(BF16), b_glu[l], g_ssm_out[l])
        y_att = _attention(proj, positions, g_attn_out[l], ssm_width)
        x = _out_projection(x, y_ssm, y_att, gt1, w_out[l].astype(BF16))
        x = _ffn(x, g_ffn[l], sc2, sh2, gt2, g_final, w_gate[l].astype(BF16),
                 w_up[l].astype(BF16), w_down[l].astype(BF16), final_norm=(l == depth - 1))
    return x
```

```python
import functools

import jax
import jax.numpy as jnp
from jax import lax
from jax.experimental import pallas as pl
from jax.experimental.pallas import tpu as pltpu

F32 = jnp.float32
BF16 = jnp.bfloat16

SSM_GROUP = 16
SSM_STATE = 64
HEAD_DIM = 128
DILATION_PAIRS = ((128, 1), (512, 4), (2048, 16))
Q_BLOCK = 128
ROPE_THETA = 10000.0
N_MOD = 6
EPS = 1e-6
NEG = -1e30

LANES = 128
SUBLANES = 8
VMEM_LIMIT = 52 * 1024 * 1024

MOD_TN = 1024
PROJ_TM = 512
PROJ_TN = 1024
FFN_TF = 512
SSM_TC = 256
PAIRS = 32
PITCH = PAIRS + 4
PAIR_VREGS = PAIRS // SUBLANES


def _params(sem):
    return pltpu.CompilerParams(dimension_semantics=sem, vmem_limit_bytes=VMEM_LIMIT)


def _const_spec(shape):
    nd = len(shape)
    return pl.BlockSpec(shape, lambda *_: (0,) * nd, pipeline_mode=pl.Buffered(1))


def _mod_kernel(c_ref, w_ref, b_ref, o_ref):
    c = c_ref[...]
    s = (c * jax.nn.sigmoid(c)).astype(BF16)
    o_ref[...] = jnp.dot(s, w_ref[...].astype(BF16), preferred_element_type=F32) + b_ref[...]


def _modulation(c_pad, w_mod, b_mod):
    rows, d = c_pad.shape
    n = w_mod.shape[1]
    return pl.pallas_call(
        _mod_kernel,
        grid=(n // MOD_TN,),
        in_specs=[
            pl.BlockSpec((rows, d), lambda j: (0, 0)),
            pl.BlockSpec((d, MOD_TN), lambda j: (0, j)),
            pl.BlockSpec((1, MOD_TN), lambda j: (0, j)),
        ],
        out_specs=pl.BlockSpec((rows, MOD_TN), lambda j: (0, j)),
        out_shape=jax.ShapeDtypeStruct((rows, n), F32),
        compiler_params=_params(("parallel",)),
        name="mod",
    )(c_pad, w_mod, b_mod.reshape(1, n))


def _modulated_norm(x, g, sc, sh):
    ms = jnp.mean(x * x, axis=-1, keepdims=True)
    return x * lax.rsqrt(ms + EPS) * (g * (1.0 + sc)) + sh


def _proj_kernel(x_ref, g_ref, sc_ref, sh_ref, w_ref, o_ref, h_ref):
    h_ref[...] = _modulated_norm(x_ref[0], g_ref[...], sc_ref[0], sh_ref[0]).astype(BF16)
    for n in range(w_ref.shape[1] // PROJ_TN):
        cols = slice(n * PROJ_TN, (n + 1) * PROJ_TN)
        o_ref[0, :, cols] = jnp.dot(h_ref[...], w_ref[:, cols],
                                    preferred_element_type=F32).astype(BF16)


def _projection(x, g, sc, sh, w_in):
    b, s, d = x.shape
    n = w_in.shape[1]
    row = lambda bi, i: (bi, i, 0)
    per_batch = lambda bi, i: (bi, 0, 0)
    return pl.pallas_call(
        _proj_kernel,
        grid=(b, s // PROJ_TM),
        in_specs=[
            pl.BlockSpec((1, PROJ_TM, d), row),
            _const_spec((1, d)),
            pl.BlockSpec((1, 1, d), per_batch),
            pl.BlockSpec((1, 1, d), per_batch),
            _const_spec((d, n)),
        ],
        out_specs=pl.BlockSpec((1, PROJ_TM, n), row),
        out_shape=jax.ShapeDtypeStruct((b, s, n), BF16),
        scratch_shapes=[pltpu.VMEM((PROJ_TM, d), BF16)],
        compiler_params=_params(("parallel", "parallel")),
        name="proj",
    )(x, g.reshape(1, d), sc, sh, w_in)


def _ssm_tables(a_re, a_im, log_dt, b_re, b_im, c_re, c_im):
    a_re = a_re.astype(F32); a_im = a_im.astype(F32)
    b_re = b_re.astype(F32); b_im = b_im.astype(F32)
    c_re = c_re.astype(F32); c_im = c_im.astype(F32)
    dt = jnp.exp(log_dt.astype(F32))[:, None]
    mag = jnp.exp(a_re * dt)
    abar_re = mag * jnp.cos(a_im * dt)
    abar_im = mag * jnp.sin(a_im * dt)
    den = a_re * a_re + a_im * a_im
    nr = abar_re - 1.0
    ni = abar_im
    f_re = (nr * a_re + ni * a_im) / den
    f_im = (ni * a_re - nr * a_im) / den
    bb_re = f_re[..., None] * b_re - f_im[..., None] * b_im
    bb_im = f_re[..., None] * b_im + f_im[..., None] * b_re

    g, p = a_re.shape
    cg = b_re.shape[-1]
    pairs = g // 2
    eye2 = jnp.eye(2, dtype=F32)

    def pair_in(t):
        t = t.transpose(0, 2, 1).reshape(pairs, 2, cg, p)
        return jnp.einsum("rkcp,kj->rkcjp", t, eye2).reshape(pairs, 2 * cg, 2 * p)

    blk = jnp.concatenate([pair_in(bb_re), pair_in(bb_im)], axis=-1)
    per_tile = LANES // (2 * cg)
    sel = jax.nn.one_hot(jnp.arange(pairs) % per_tile, per_tile, dtype=F32)
    bb = jnp.einsum("rcn,rq->rqcn", blk, sel).reshape(pairs, LANES, 4 * p)

    def pair_out(t):
        t = t.transpose(0, 2, 1).reshape(pairs, 2, p, cg)
        return jnp.einsum("rkpc,kj->rkpjc", t, eye2).reshape(pairs, 2 * p, 2 * cg)

    cblk = jnp.concatenate([pair_out(c_re), pair_out(-c_im)], axis=1)
    per_out = (2 * LANES) // (2 * cg)
    osel = jax.nn.one_hot(jnp.arange(pairs) % per_out, per_out, dtype=F32)
    cc = jnp.einsum("rkc,rq->rkqc", cblk, osel).reshape(pairs, 4 * p, 2 * LANES)

    def scan_rows(t):
        return t.reshape(SUBLANES, PAIR_VREGS, LANES).transpose(1, 0, 2)

    return bb.astype(BF16), cc.astype(BF16), scan_rows(abar_re), scan_rows(abar_im)


def _ssm_kernel(u_ref, bb_ref, ar_ref, ai_ref, cc_ref, d_ref, wglu_ref, bglu_ref, g_ref,
                o_ref, zre, zim, st, y_s):
    tc = u_ref.shape[1]

    @pl.when(pl.program_id(1) == 0)
    def _():
        st[...] = jnp.zeros_like(st)

    pairs_per_tile = PAIRS // (u_ref.shape[2] // LANES)
    for r in range(PAIRS):
        j = r // pairs_per_tile
        bu = jnp.dot(u_ref[0, :, j * LANES:(j + 1) * LANES], bb_ref[r],
                     preferred_element_type=F32)
        zre[pl.ds(r, tc, stride=PITCH), :] = bu[:, :LANES]
        zim[pl.ds(r, tc, stride=PITCH), :] = bu[:, LANES:]

    ar = [ar_ref[k] for k in range(PAIR_VREGS)]
    ai = [ai_ref[k] for k in range(PAIR_VREGS)]

    def step(t, carry):
        xr, xi = carry
        base = t * PITCH
        nxr, nxi = [], []
        for k in range(PAIR_VREGS):
            rows = pl.ds(base + k, SUBLANES, stride=PAIR_VREGS)
            nr = ar[k] * xr[k] - ai[k] * xi[k] + zre[rows, :]
            ni = ar[k] * xi[k] + ai[k] * xr[k] + zim[rows, :]
            zre[rows, :] = nr
            zim[rows, :] = ni
            nxr.append(nr)
            nxi.append(ni)
        return tuple(nxr), tuple(nxi)

    carry0 = (tuple(st[0, k] for k in range(PAIR_VREGS)),
              tuple(st[1, k] for k in range(PAIR_VREGS)))
    xr, xi = lax.fori_loop(0, tc, step, carry0, unroll=4)
    for k in range(PAIR_VREGS):
        st[0, k] = xr[k]
        st[1, k] = xi[k]

    out_tile = 2 * LANES
    pairs_per_out = PAIRS // (o_ref.shape[2] // out_tile)
    for j in range(o_ref.shape[2] // out_tile):
        acc = None
        for q in range(pairs_per_out):
            r = j * pairs_per_out + q
            lhs = jnp.concatenate([zre[pl.ds(r, tc, stride=PITCH), :].astype(BF16),
                                   zim[pl.ds(r, tc, stride=PITCH), :].astype(BF16)], axis=1)
            part = jnp.dot(lhs, cc_ref[r], preferred_element_type=F32)
            acc = part if acc is None else acc + part
        cols = slice(j * out_tile, (j + 1) * out_tile)
        y = acc + d_ref[:, cols] * u_ref[0, :, cols].astype(F32)
        y_s[:, cols] = jax.nn.gelu(y)

    vg = y_s[...]
    z = jnp.dot(vg.astype(BF16), wglu_ref[...], preferred_element_type=F32) + bglu_ref[...]
    out = vg * jax.nn.sigmoid(z)
    ms = jnp.mean(out * out, axis=-1, keepdims=True)
    o_ref[0] = (out * lax.rsqrt(ms + EPS) * g_ref[...]).astype(BF16)


def _ssm(proj, tables, d_skip, w_glu, b_glu, g_out):
    b, s, _ = proj.shape
    bb, cc, ar, ai = tables
    w = w_glu.shape[0]
    assert PAIRS * 2 * SSM_GROUP == w
    return pl.pallas_call(
        _ssm_kernel,
        grid=(b, s // SSM_TC),
        in_specs=[
            pl.BlockSpec((1, SSM_TC, w), lambda bi, t: (bi, t, 0)),
            _const_spec(bb.shape),
            _const_spec(ar.shape),
            _const_spec(ai.shape),
            _const_spec(cc.shape),
            _const_spec((1, w)),
            _const_spec((w, w)),
            _const_spec((1, w)),
            _const_spec((1, w)),
        ],
        out_specs=pl.BlockSpec((1, SSM_TC, w), lambda bi, t: (bi, t, 0)),
        out_shape=jax.ShapeDtypeStruct((b, s, w), BF16),
        scratch_shapes=[
            pltpu.VMEM((SSM_TC * PITCH, LANES), F32),
            pltpu.VMEM((SSM_TC * PITCH, LANES), F32),
            pltpu.VMEM((2, PAIR_VREGS, SUBLANES, LANES), F32),
            pltpu.VMEM((SSM_TC, w), F32),
        ],
        compiler_params=_params(("parallel", "arbitrary")),
        name="ssm",
    )(proj, bb, ar, ai, cc, d_skip.reshape(1, w), w_glu, b_glu.reshape(1, w),
      g_out.reshape(1, w))


def _attn_kernel(pos_ref, invf_ref, q_ref, k_ref, v_ref, g_ref, o_ref,
                 cos_s, sin_s, f_s, p_s, o_s, l_s, y_s):
    s = q_ref.shape[1]
    n_heads = y_s.shape[0]
    h = pl.program_id(1)
    half = HEAD_DIM // 2

    @pl.when(h == 0)
    def _():
        ang = pos_ref[0].astype(F32) * invf_ref[...]
        lane = lax.broadcasted_iota(jnp.int32, (1, HEAD_DIM), 1)
        cos_s[...] = jnp.cos(ang)
        sin_s[...] = jnp.sin(ang) * jnp.where(lane < half, -1.0, 1.0)

    def rope(t):
        return t * cos_s[...] + pltpu.roll(t, half, 1) * sin_s[...]

    f_s[0] = rope(q_ref[0].astype(F32)) * (HEAD_DIM ** -0.5)
    f_s[1] = rope(k_ref[0].astype(F32))
    f_s[2] = v_ref[0].astype(F32)

    for bi, (_, d) in enumerate(DILATION_PAIRS):
        seg = s // d
        for j in range(3):
            for r in range(d):
                src = f_s[j] if d == 1 else f_s[j, pl.ds(r, seg, stride=d), :]
                p_s[bi, j, r * seg:(r + 1) * seg, :] = src.astype(BF16)

    def banded_mask(n_keys, steps):
        row = lax.broadcasted_iota(jnp.int32, (Q_BLOCK, n_keys), 0)
        col = lax.broadcasted_iota(jnp.int32, (Q_BLOCK, n_keys), 1)
        dist = row + (n_keys - Q_BLOCK) - col
        return (dist >= 0) & (dist <= steps)

    for bi, (window, d) in enumerate(DILATION_PAIRS):
        steps = window // d
        span = Q_BLOCK * d
        nb = s // span
        masks = {nk: banded_mask(nk, steps) for nk in (Q_BLOCK, 2 * Q_BLOCK)}
        for i in range(s // Q_BLOCK):
            r, n = divmod(i, nb)
            lo = (i - 1) * Q_BLOCK if n > 0 else i * Q_BLOCK
            hi = (i + 1) * Q_BLOCK
            q = p_s[bi, 0, i * Q_BLOCK:hi, :]
            sc = lax.dot_general(q, p_s[bi, 1, lo:hi, :], (((1,), (1,)), ((), ())),
                                 preferred_element_type=F32)
            sc = jnp.where(masks[hi - lo], sc, NEG)
            m = jnp.max(sc, axis=-1, keepdims=True)
            p = jnp.exp(sc - m)
            l = jnp.sum(p, axis=-1, keepdims=True)
            pv = jnp.dot(p.astype(BF16), p_s[bi, 2, lo:hi, :], preferred_element_type=F32)
            start = r + n * span
            dst = pl.ds(start, Q_BLOCK) if d == 1 else pl.ds(start, Q_BLOCK, stride=d)
            o_s[bi, dst, :] = pv * (1.0 / l)
            l_s[bi, dst, :] = jnp.broadcast_to(m + jnp.log(l), (Q_BLOCK, HEAD_DIM))

    lses = [l_s[bi] for bi in range(len(DILATION_PAIRS))]
    top = functools.reduce(jnp.maximum, lses)
    es = [jnp.exp(x - top) for x in lses]
    num = sum(e * o_s[bi] for bi, e in enumerate(es))
    y_s[h] = num / sum(es)

    @pl.when(h == n_heads - 1)
    def _():
        ss = sum(jnp.sum(y_s[i] * y_s[i], axis=-1, keepdims=True) for i in range(n_heads))
        inv = lax.rsqrt(ss / (n_heads * HEAD_DIM) + EPS)
        for i in range(n_heads):
            cols = slice(i * HEAD_DIM, (i + 1) * HEAD_DIM)
            o_ref[0, :, cols] = (y_s[i] * inv * g_ref[:, cols]).astype(BF16)


def _attention(proj, positions, g_out, ssm_width):
    b, s, _ = proj.shape
    w = g_out.shape[0]
    n_heads = w // HEAD_DIM
    half = HEAD_DIM // 2
    inv_freq = ROPE_THETA ** (-jnp.arange(half, dtype=F32) / half)
    inv_freq = jnp.concatenate([inv_freq, inv_freq]).reshape(1, HEAD_DIM)
    first = ssm_width // HEAD_DIM

    def head_cols(offset):
        return pl.BlockSpec((1, s, HEAD_DIM), lambda bi, h: (bi, 0, first + offset * n_heads + h))

    n_br = len(DILATION_PAIRS)
    return pl.pallas_call(
        _attn_kernel,
        grid=(b, n_heads),
        in_specs=[
            pl.BlockSpec((1, s, 1), lambda bi, h: (bi, 0, 0)),
            _const_spec((1, HEAD_DIM)),
            head_cols(0), head_cols(1), head_cols(2),
            _const_spec((1, w)),
        ],
        out_specs=pl.BlockSpec((1, s, w), lambda bi, h: (bi, 0, 0)),
        out_shape=jax.ShapeDtypeStruct((b, s, w), BF16),
        scratch_shapes=[
            pltpu.VMEM((s, HEAD_DIM), F32),
            pltpu.VMEM((s, HEAD_DIM), F32),
            pltpu.VMEM((3, s, HEAD_DIM), F32),
            pltpu.VMEM((n_br, 3, s, HEAD_DIM), BF16),
            pltpu.VMEM((n_br, s, HEAD_DIM), F32),
            pltpu.VMEM((n_br, s, HEAD_DIM), F32),
            pltpu.VMEM((n_heads, s, HEAD_DIM), F32),
        ],
        compiler_params=_params(("parallel", "arbitrary")),
        name="attn",
    )(positions.reshape(b, s, 1), inv_freq, proj, proj, proj, g_out.reshape(1, w))


def _outproj_kernel(x_ref, ys_ref, ya_ref, gt_ref, w_ref, o_ref):
    ws = ys_ref.shape[2]
    acc = jnp.dot(ys_ref[0], w_ref[:ws, :], preferred_element_type=F32)
    acc = acc + jnp.dot(ya_ref[0], w_ref[ws:, :], preferred_element_type=F32)
    o_ref[0] = x_ref[0] + gt_ref[0] * acc


def _out_projection(x, y_ssm, y_att, gt, w_out):
    b, s, d = x.shape
    row = lambda bi, i: (bi, i, 0)
    return pl.pallas_call(
        _outproj_kernel,
        grid=(b, s // PROJ_TM),
        in_specs=[
            pl.BlockSpec((1, PROJ_TM, d), row),
            pl.BlockSpec((1, PROJ_TM, y_ssm.shape[2]), row),
            pl.BlockSpec((1, PROJ_TM, y_att.shape[2]), row),
            pl.BlockSpec((1, 1, d), lambda bi, i: (bi, 0, 0)),
            _const_spec(w_out.shape),
        ],
        out_specs=pl.BlockSpec((1, PROJ_TM, d), row),
        out_shape=jax.ShapeDtypeStruct((b, s, d), F32),
        compiler_params=_params(("parallel", "parallel")),
        name="outproj",
    )(x, y_ssm, y_att, gt, w_out)


def _ffn_kernel(x_ref, g_ref, sc_ref, sh_ref, gt_ref, gf_ref, wg_ref, wu_ref, wd_ref,
                o_ref, h_ref, acc_ref, *, final_norm):
    j = pl.program_id(2)

    @pl.when(j == 0)
    def _():
        h_ref[...] = _modulated_norm(x_ref[0], g_ref[...], sc_ref[0], sh_ref[0]).astype(BF16)
        acc_ref[...] = jnp.zeros_like(acc_ref)

    h = h_ref[...]
    gate = jnp.dot(h, wg_ref[...], preferred_element_type=F32)
    up = jnp.dot(h, wu_ref[...], preferred_element_type=F32)
    act = (gate * jax.nn.sigmoid(gate) * up).astype(BF16)
    acc_ref[...] += jnp.dot(act, wd_ref[...], preferred_element_type=F32)

    @pl.when(j == pl.num_programs(2) - 1)
    def _():
        y = x_ref[0] + gt_ref[0] * acc_ref[...]
        if final_norm:
            ms = jnp.mean(y * y, axis=-1, keepdims=True)
            y = y * lax.rsqrt(ms + EPS) * gf_ref[...]
        o_ref[0] = y


def _ffn(x, g, sc, sh, gt, g_final, w_gate, w_up, w_down, final_norm):
    b, s, d = x.shape
    dff = w_gate.shape[1]
    row = lambda bi, i, j: (bi, i, 0)
    per_batch = lambda bi, i, j: (bi, 0, 0)
    return pl.pallas_call(
        functools.partial(_ffn_kernel, final_norm=final_norm),
        grid=(b, s // PROJ_TM, dff // FFN_TF),
        in_specs=[
            pl.BlockSpec((1, PROJ_TM, d), row),
            _const_spec((1, d)),
            pl.BlockSpec((1, 1, d), per_batch),
            pl.BlockSpec((1, 1, d), per_batch),
            pl.BlockSpec((1, 1, d), per_batch),
            _const_spec((1, d)),
            pl.BlockSpec((d, FFN_TF), lambda bi, i, j: (0, j)),
            pl.BlockSpec((d, FFN_TF), lambda bi, i, j: (0, j)),
            pl.BlockSpec((FFN_TF, d), lambda bi, i, j: (j, 0)),
        ],
        out_specs=pl.BlockSpec((1, PROJ_TM, d), row),
        out_shape=jax.ShapeDtypeStruct((b, s, d), F32),
        scratch_shapes=[pltpu.VMEM((PROJ_TM, d), BF16), pltpu.VMEM((PROJ_TM, d), F32)],
        compiler_params=_params(("parallel", "parallel", "arbitrary")),
        name="ffn",
    )(x, g.reshape(1, d), sc, sh, gt, g_final.reshape(1, d), w_gate, w_up, w_down)


def kernel(x, c, positions, w_mod, b_mod, g_mix, w_in, ssm_a_re, ssm_a_im, ssm_log_dt,
           ssm_b_re, ssm_b_im, ssm_c_re, ssm_c_im, ssm_d, w_glu, b_glu, g_ssm_out,
           g_attn_out, w_out, g_ffn, w_gate, w_up, w_down, g_final):
    depth = w_mod.shape[0]
    b, s, d = x.shape
    ssm_width = w_glu.shape[1]
    c_pad = jnp.zeros((SUBLANES, d), F32).at[:b].set(c.astype(F32))
    x = x.astype(F32)
    for l in range(depth):
        mod = _modulation(c_pad, w_mod[l], b_mod[l])[:b]
        sh1, sc1, gt1, sh2, sc2, gt2 = [m.reshape(b, 1, d) for m in jnp.split(mod, N_MOD, axis=-1)]
        proj = _projection(x, g_mix[l], sc1, sh1, w_in[l].astype(BF16))
        tables = _ssm_tables(ssm_a_re[l], ssm_a_im[l], ssm_log_dt[l], ssm_b_re[l], ssm_b_im[l],
                             ssm_c_re[l], ssm_c_im[l])
        y_ssm = _ssm(proj, tables, ssm_d[l], w_glu[l].astype(BF16), b_glu[l], g_ssm_out[l])
        y_att = _attention(proj, positions, g_attn_out[l], ssm_width)
        x = _out_projection(x, y_ssm, y_att, gt1, w_out[l].astype(BF16))
        x = _ffn(x, g_ffn[l], sc2, sh2, gt2, g_final, w_gate[l].astype(BF16),
                 w_up[l].astype(BF16), w_down[l].astype(BF16), final_norm=(l == depth - 1))
    return x
```

```python
import functools

import jax
import jax.numpy as jnp
from jax import lax
from jax.experimental import pallas as pl
from jax.experimental.pallas import tpu as pltpu

F32 = jnp.float32
BF16 = jnp.bfloat16

SSM_GROUP = 16
SSM_STATE = 64
HEAD_DIM = 128
DILATION_PAIRS = ((128, 1), (512, 4), (2048, 16))
Q_BLOCK = 128
ROPE_THETA = 10000.0
N_MOD = 6
EPS = 1e-6
NEG = -1e30

LANES = 128
SUBLANES = 8
BF16_ROWS = 16
VMEM_LIMIT = 60 * 1024 * 1024

MOD_TN = 1024
PROJ_TM = 512
PROJ_TN = 1024
FFN_TF = 512
SSM_TC = 512
PAIRS = 32
PITCH = PAIRS + 4
PAIR_VREGS = PAIRS // SUBLANES


def _params(sem):
    return pltpu.CompilerParams(dimension_semantics=sem, vmem_limit_bytes=VMEM_LIMIT)


def _const_spec(shape):
    nd = len(shape)
    return pl.BlockSpec(shape, lambda *_: (0,) * nd, pipeline_mode=pl.Buffered(1))


def _mod_kernel(c_ref, w_ref, b_ref, o_ref):
    c = c_ref[...]
    s = (c * jax.nn.sigmoid(c)).astype(BF16)
    o_ref[...] = jnp.dot(s, w_ref[...].astype(BF16), preferred_element_type=F32) + b_ref[...]


def _modulation(c_pad, w_mod, b_mod):
    rows, d = c_pad.shape
    n = w_mod.shape[1]
    return pl.pallas_call(
        _mod_kernel,
        grid=(n // MOD_TN,),
        in_specs=[
            pl.BlockSpec((rows, d), lambda j: (0, 0)),
            pl.BlockSpec((d, MOD_TN), lambda j: (0, j)),
            pl.BlockSpec((1, MOD_TN), lambda j: (0, j)),
        ],
        out_specs=pl.BlockSpec((rows, MOD_TN), lambda j: (0, j)),
        out_shape=jax.ShapeDtypeStruct((rows, n), F32),
        compiler_params=_params(("parallel",)),
        name="mod",
    )(c_pad, w_mod, b_mod.reshape(1, n))


def _modulated_norm(x, g, sc, sh):
    ms = jnp.mean(x * x, axis=-1, keepdims=True)
    return x * lax.rsqrt(ms + EPS) * (g * (1.0 + sc)) + sh


def _proj_kernel(x_ref, g_ref, sc_ref, sh_ref, w_ref, o_ref, h_ref):
    h_ref[...] = _modulated_norm(x_ref[0], g_ref[...], sc_ref[0], sh_ref[0]).astype(BF16)
    for n in range(w_ref.shape[1] // PROJ_TN):
        cols = slice(n * PROJ_TN, (n + 1) * PROJ_TN)
        o_ref[0, :, cols] = jnp.dot(h_ref[...], w_ref[:, cols],
                                    preferred_element_type=F32).astype(BF16)


def _projection(x, g, sc, sh, w_in):
    b, s, d = x.shape
    n = w_in.shape[1]
    row = lambda bi, i: (bi, i, 0)
    per_batch = lambda bi, i: (bi, 0, 0)
    return pl.pallas_call(
        _proj_kernel,
        grid=(b, s // PROJ_TM),
        in_specs=[
            pl.BlockSpec((1, PROJ_TM, d), row),
            _const_spec((1, d)),
            pl.BlockSpec((1, 1, d), per_batch),
            pl.BlockSpec((1, 1, d), per_batch),
            _const_spec((d, n)),
        ],
        out_specs=pl.BlockSpec((1, PROJ_TM, n), row),
        out_shape=jax.ShapeDtypeStruct((b, s, n), BF16),
        scratch_shapes=[pltpu.VMEM((PROJ_TM, d), BF16)],
        compiler_params=_params(("parallel", "parallel")),
        name="proj",
    )(x, g.reshape(1, d), sc, sh, w_in)


def _ssm_tables(a_re, a_im, log_dt, b_re, b_im, c_re, c_im):
    a_re = a_re.astype(F32); a_im = a_im.astype(F32)
    b_re = b_re.astype(F32); b_im = b_im.astype(F32)
    c_re = c_re.astype(F32); c_im = c_im.astype(F32)
    dt = jnp.exp(log_dt.astype(F32))[:, None]
    mag = jnp.exp(a_re * dt)
    abar_re = mag * jnp.cos(a_im * dt)
    abar_im = mag * jnp.sin(a_im * dt)
    den = a_re * a_re + a_im * a_im
    nr = abar_re - 1.0
    ni = abar_im
    f_re = (nr * a_re + ni * a_im) / den
    f_im = (ni * a_re - nr * a_im) / den
    bb_re = f_re[..., None] * b_re - f_im[..., None] * b_im
    bb_im = f_re[..., None] * b_im + f_im[..., None] * b_re

    g, p = a_re.shape
    cg = b_re.shape[-1]
    pairs = g // 2
    eye2 = jnp.eye(2, dtype=F32)

    def pair_in(t):
        t = t.transpose(0, 2, 1).reshape(pairs, 2, cg, p)
        return jnp.einsum("rkcp,kj->rkcjp", t, eye2).reshape(pairs, 2 * cg, 2 * p)

    blk = jnp.concatenate([pair_in(bb_re), pair_in(bb_im)], axis=-1)
    per_tile = LANES // (2 * cg)
    sel = jax.nn.one_hot(jnp.arange(pairs) % per_tile, per_tile, dtype=F32)
    bb = jnp.einsum("rcn,rq->rqcn", blk, sel).reshape(pairs, LANES, 4 * p)

    def pair_out(t):
        t = t.transpose(0, 2, 1).reshape(pairs, 2, p, cg)
        return jnp.einsum("rkpc,kj->rkpjc", t, eye2).reshape(pairs, 2 * p, 2 * cg)

    cblk = jnp.concatenate([pair_out(c_re), pair_out(-c_im)], axis=1)
    per_out = (2 * LANES) // (2 * cg)
    osel = jax.nn.one_hot(jnp.arange(pairs) % per_out, per_out, dtype=F32)
    cc = jnp.einsum("rkc,rq->rkqc", cblk, osel).reshape(pairs, 4 * p, 2 * LANES)

    def scan_rows(t):
        return t.reshape(SUBLANES, PAIR_VREGS, LANES).transpose(1, 0, 2)

    return bb.astype(BF16), cc.astype(BF16), scan_rows(abar_re), scan_rows(abar_im)


def _ssm_kernel(u_ref, bb_ref, ar_ref, ai_ref, cc_ref, d_ref, wglu_ref, bglu_ref, g_ref,
                o_ref, zre, zim, st, y_s):
    tc = u_ref.shape[1]

    @pl.when(pl.program_id(1) == 0)
    def _():
        st[...] = jnp.zeros_like(st)

    pairs_per_tile = PAIRS // (u_ref.shape[2] // LANES)
    for r in range(PAIRS):
        j = r // pairs_per_tile
        bu = jnp.dot(u_ref[0, :, j * LANES:(j + 1) * LANES], bb_ref[r],
                     preferred_element_type=F32)
        zre[pl.ds(r, tc, stride=PITCH), :] = bu[:, :LANES]
        zim[pl.ds(r, tc, stride=PITCH), :] = bu[:, LANES:]

    ar = [ar_ref[k] for k in range(PAIR_VREGS)]
    ai = [ai_ref[k] for k in range(PAIR_VREGS)]

    def step(t, carry):
        xr, xi = carry
        base = t * PITCH
        nxr, nxi = [], []
        for k in range(PAIR_VREGS):
            rows = pl.ds(base + k, SUBLANES, stride=PAIR_VREGS)
            nr = ar[k] * xr[k] - ai[k] * xi[k] + zre[rows, :]
            ni = ar[k] * xi[k] + ai[k] * xr[k] + zim[rows, :]
            zre[rows, :] = nr
            zim[rows, :] = ni
            nxr.append(nr)
            nxi.append(ni)
        return tuple(nxr), tuple(nxi)

    carry0 = (tuple(st[0, k] for k in range(PAIR_VREGS)),
              tuple(st[1, k] for k in range(PAIR_VREGS)))
    xr, xi = lax.fori_loop(0, tc, step, carry0, unroll=4)
    for k in range(PAIR_VREGS):
        st[0, k] = xr[k]
        st[1, k] = xi[k]

    out_tile = 2 * LANES
    pairs_per_out = PAIRS // (o_ref.shape[2] // out_tile)
    for j in range(o_ref.shape[2] // out_tile):
        acc = None
        for q in range(pairs_per_out):
            r = j * pairs_per_out + q
            lhs = jnp.concatenate([zre[pl.ds(r, tc, stride=PITCH), :].astype(BF16),
                                   zim[pl.ds(r, tc, stride=PITCH), :].astype(BF16)], axis=1)
            part = jnp.dot(lhs, cc_ref[r], preferred_element_type=F32)
            acc = part if acc is None else acc + part
        cols = slice(j * out_tile, (j + 1) * out_tile)
        y = acc + d_ref[:, cols] * u_ref[0, :, cols].astype(F32)
        y_s[:, cols] = jax.nn.gelu(y)

    vg = y_s[...]
    z = jnp.dot(vg.astype(BF16), wglu_ref[...], preferred_element_type=F32) + bglu_ref[...]
    out = vg * jax.nn.sigmoid(z)
    ms = jnp.mean(out * out, axis=-1, keepdims=True)
    o_ref[0] = (out * lax.rsqrt(ms + EPS) * g_ref[...]).astype(BF16)


def _ssm(proj, tables, d_skip, w_glu, b_glu, g_out):
    b, s, _ = proj.shape
    bb, cc, ar, ai = tables
    w = w_glu.shape[0]
    assert PAIRS * 2 * SSM_GROUP == w
    return pl.pallas_call(
        _ssm_kernel,
        grid=(b, s // SSM_TC),
        in_specs=[
            pl.BlockSpec((1, SSM_TC, w), lambda bi, t: (bi, t, 0)),
            _const_spec(bb.shape),
            _const_spec(ar.shape),
            _const_spec(ai.shape),
            _const_spec(cc.shape),
            _const_spec((1, w)),
            _const_spec((w, w)),
            _const_spec((1, w)),
            _const_spec((1, w)),
        ],
        out_specs=pl.BlockSpec((1, SSM_TC, w), lambda bi, t: (bi, t, 0)),
        out_shape=jax.ShapeDtypeStruct((b, s, w), BF16),
        scratch_shapes=[
            pltpu.VMEM((SSM_TC * PITCH, LANES), F32),
            pltpu.VMEM((SSM_TC * PITCH, LANES), F32),
            pltpu.VMEM((2, PAIR_VREGS, SUBLANES, LANES), F32),
            pltpu.VMEM((SSM_TC, w), F32),
        ],
        compiler_params=_params(("parallel", "arbitrary")),
        name="ssm",
    )(proj, bb, ar, ai, cc, d_skip.reshape(1, w), w_glu, b_glu.reshape(1, w),
      g_out.reshape(1, w))


def _attn_kernel(*refs, n_cast):
    (pos_ref, invf_ref, q_ref, k_ref, v_ref, g_ref), refs = refs[:6], refs[6:]
    cast_in, refs = refs[:n_cast], refs[n_cast:]
    o_ref, refs = refs[0], refs[1:]
    cast_out, refs = refs[:n_cast], refs[n_cast:]
    cos_s, sin_s, f_s, p_s, o_s, l_s, y_s = refs

    s = q_ref.shape[1]
    n_heads = y_s.shape[0]
    h = pl.program_id(1)
    half = HEAD_DIM // 2

    for src, dst in zip(cast_in, cast_out):
        dst[...] = src[...].astype(BF16)

    @pl.when(h == 0)
    def _():
        ang = pos_ref[0].astype(F32) * invf_ref[...]
        lane = lax.broadcasted_iota(jnp.int32, (1, HEAD_DIM), 1)
        cos_s[...] = jnp.cos(ang)
        sin_s[...] = jnp.sin(ang) * jnp.where(lane < half, -1.0, 1.0)

    def rope(t):
        return t * cos_s[...] + pltpu.roll(t, half, 1) * sin_s[...]

    f_s[0] = rope(q_ref[0].astype(F32)) * (HEAD_DIM ** -0.5)
    f_s[1] = rope(k_ref[0].astype(F32))
    f_s[2] = v_ref[0].astype(F32)

    for bi, (_, d) in enumerate(DILATION_PAIRS):
        seg = s // d
        for j in range(3):
            for r in range(d):
                src = f_s[j] if d == 1 else f_s[j, pl.ds(r, seg, stride=d), :]
                p_s[bi, j, r * seg:(r + 1) * seg, :] = src.astype(BF16)

    def banded_mask(n_keys, steps):
        row = lax.broadcasted_iota(jnp.int32, (Q_BLOCK, n_keys), 0)
        col = lax.broadcasted_iota(jnp.int32, (Q_BLOCK, n_keys), 1)
        dist = row + (n_keys - Q_BLOCK) - col
        return (dist >= 0) & (dist <= steps)

    for bi, (window, d) in enumerate(DILATION_PAIRS):
        steps = window // d
        span = Q_BLOCK * d
        nb = s // span
        masks = {nk: banded_mask(nk, steps) for nk in (Q_BLOCK, 2 * Q_BLOCK)}
        for i in range(s // Q_BLOCK):
            r, n = divmod(i, nb)
            lo = (i - 1) * Q_BLOCK if n > 0 else i * Q_BLOCK
            hi = (i + 1) * Q_BLOCK
            q = p_s[bi, 0, i * Q_BLOCK:hi, :]
            sc = lax.dot_general(q, p_s[bi, 1, lo:hi, :], (((1,), (1,)), ((), ())),
                                 preferred_element_type=F32)
            sc = jnp.where(masks[hi - lo], sc, NEG)
            m = jnp.max(sc, axis=-1, keepdims=True)
            p = jnp.exp(sc - m)
            l = jnp.sum(p, axis=-1, keepdims=True)
            pv = jnp.dot(p.astype(BF16), p_s[bi, 2, lo:hi, :], preferred_element_type=F32)
            start = r + n * span
            dst = pl.ds(start, Q_BLOCK) if d == 1 else pl.ds(start, Q_BLOCK, stride=d)
            o_s[bi, dst, :] = pv * (1.0 / l)
            l_s[bi, dst, :] = jnp.broadcast_to(m + jnp.log(l), (Q_BLOCK, HEAD_DIM))

    lses = [l_s[bi] for bi in range(len(DILATION_PAIRS))]
    top = functools.reduce(jnp.maximum, lses)
    es = [jnp.exp(x - top) for x in lses]
    num = sum(e * o_s[bi] for bi, e in enumerate(es))
    y_s[h] = num / sum(es)

    @pl.when(h == n_heads - 1)
    def _():
        ss = sum(jnp.sum(y_s[i] * y_s[i], axis=-1, keepdims=True) for i in range(n_heads))
        inv = lax.rsqrt(ss / (n_heads * HEAD_DIM) + EPS)
        for i in range(n_heads):
            cols = slice(i * HEAD_DIM, (i + 1) * HEAD_DIM)
            o_ref[0, :, cols] = (y_s[i] * inv * g_ref[:, cols]).astype(BF16)


def _attention(proj, positions, g_out, ssm_width, weights):
    b, s, _ = proj.shape
    w = g_out.shape[0]
    n_heads = w // HEAD_DIM
    half = HEAD_DIM // 2
    inv_freq = ROPE_THETA ** (-jnp.arange(half, dtype=F32) / half)
    inv_freq = jnp.concatenate([inv_freq, inv_freq]).reshape(1, HEAD_DIM)
    first = ssm_width // HEAD_DIM
    steps = b * n_heads

    def head_cols(offset):
        return pl.BlockSpec((1, s, HEAD_DIM), lambda bi, h: (bi, 0, first + offset * n_heads + h))

    def slab(wt):
        rows = wt.shape[0] // steps
        assert rows * steps == wt.shape[0] and rows % BF16_ROWS == 0
        return pl.BlockSpec((rows, wt.shape[1]), lambda bi, h: (bi * n_heads + h, 0))

    n_br = len(DILATION_PAIRS)
    outs = pl.pallas_call(
        functools.partial(_attn_kernel, n_cast=len(weights)),
        grid=(b, n_heads),
        in_specs=[
            pl.BlockSpec((1, s, 1), lambda bi, h: (bi, 0, 0)),
            _const_spec((1, HEAD_DIM)),
            head_cols(0), head_cols(1), head_cols(2),
            _const_spec((1, w)),
        ] + [slab(wt) for wt in weights],
        out_specs=[pl.BlockSpec((1, s, w), lambda bi, h: (bi, 0, 0), pipeline_mode=pl.Buffered(1))]
        + [slab(wt) for wt in weights],
        out_shape=[jax.ShapeDtypeStruct((b, s, w), BF16)]
        + [jax.ShapeDtypeStruct(wt.shape, BF16) for wt in weights],
        scratch_shapes=[
            pltpu.VMEM((s, HEAD_DIM), F32),
            pltpu.VMEM((s, HEAD_DIM), F32),
            pltpu.VMEM((3, s, HEAD_DIM), F32),
            pltpu.VMEM((n_br, 3, s, HEAD_DIM), BF16),
            pltpu.VMEM((n_br, s, HEAD_DIM), F32),
            pltpu.VMEM((n_br, s, HEAD_DIM), F32),
            pltpu.VMEM((n_heads, s, HEAD_DIM), F32),
        ],
        compiler_params=_params(("parallel", "arbitrary")),
        name="attn",
    )(positions.reshape(b, s, 1), inv_freq, proj, proj, proj, g_out.reshape(1, w), *weights)
    return outs[0], outs[1:]


def _outproj_kernel(x_ref, ys_ref, ya_ref, gt_ref, w_ref, o_ref):
    ws = ys_ref.shape[2]
    acc = jnp.dot(ys_ref[0], w_ref[:ws, :], preferred_element_type=F32)
    acc = acc + jnp.dot(ya_ref[0], w_ref[ws:, :], preferred_element_type=F32)
    o_ref[0] = x_ref[0] + gt_ref[0] * acc


def _out_projection(x, y_ssm, y_att, gt, w_out):
    b, s, d = x.shape
    row = lambda bi, i: (bi, i, 0)
    return pl.pallas_call(
        _outproj_kernel,
        grid=(b, s // PROJ_TM),
        in_specs=[
            pl.BlockSpec((1, PROJ_TM, d), row),
            pl.BlockSpec((1, PROJ_TM, y_ssm.shape[2]), row),
            pl.BlockSpec((1, PROJ_TM, y_att.shape[2]), row),
            pl.BlockSpec((1, 1, d), lambda bi, i: (bi, 0, 0)),
            _const_spec(w_out.shape),
        ],
        out_specs=pl.BlockSpec((1, PROJ_TM, d), row),
        out_shape=jax.ShapeDtypeStruct((b, s, d), F32),
        compiler_params=_params(("parallel", "parallel")),
        name="outproj",
    )(x, y_ssm, y_att, gt, w_out)


def _ffn_kernel(x_ref, g_ref, sc_ref, sh_ref, gt_ref, gf_ref, wg_ref, wu_ref, wd_ref,
                o_ref, h_ref, acc_ref, *, final_norm):
    j = pl.program_id(2)

    @pl.when(j == 0)
    def _():
        h_ref[...] = _modulated_norm(x_ref[0], g_ref[...], sc_ref[0], sh_ref[0]).astype(BF16)
        acc_ref[...] = jnp.zeros_like(acc_ref)

    h = h_ref[...]
    gate = jnp.dot(h, wg_ref[...], preferred_element_type=F32)
    up = jnp.dot(h, wu_ref[...], preferred_element_type=F32)
    act = (gate * jax.nn.sigmoid(gate) * up).astype(BF16)
    acc_ref[...] += jnp.dot(act, wd_ref[...], preferred_element_type=F32)

    @pl.when(j == pl.num_programs(2) - 1)
    def _():
        y = x_ref[0] + gt_ref[0] * acc_ref[...]
        if final_norm:
            ms = jnp.mean(y * y, axis=-1, keepdims=True)
            y = y * lax.rsqrt(ms + EPS) * gf_ref[...]
        o_ref[0] = y


def _ffn(x, g, sc, sh, gt, g_final, w_gate, w_up, w_down, final_norm):
    b, s, d = x.shape
    dff = w_gate.shape[1]
    row = lambda bi, i, j: (bi, i, 0)
    per_batch = lambda bi, i, j: (bi, 0, 0)
    return pl.pallas_call(
        functools.partial(_ffn_kernel, final_norm=final_norm),
        grid=(b, s // PROJ_TM, dff // FFN_TF),
        in_specs=[
            pl.BlockSpec((1, PROJ_TM, d), row),
            _const_spec((1, d)),
            pl.BlockSpec((1, 1, d), per_batch),
            pl.BlockSpec((1, 1, d), per_batch),
            pl.BlockSpec((1, 1, d), per_batch),
            _const_spec((1, d)),
            pl.BlockSpec((d, FFN_TF), lambda bi, i, j: (0, j)),
            pl.BlockSpec((d, FFN_TF), lambda bi, i, j: (0, j)),
            pl.BlockSpec((FFN_TF, d), lambda bi, i, j: (j, 0)),
        ],
        out_specs=pl.BlockSpec((1, PROJ_TM, d), row),
        out_shape=jax.ShapeDtypeStruct((b, s, d), F32),
        scratch_shapes=[pltpu.VMEM((PROJ_TM, d), BF16), pltpu.VMEM((PROJ_TM, d), F32)],
        compiler_params=_params(("parallel", "parallel", "arbitrary")),
        name="ffn",
    )(x, g.reshape(1, d), sc, sh, gt, g_final.reshape(1, d), w_gate, w_up, w_down)


def kernel(x, c, positions, w_mod, b_mod, g_mix, w_in, ssm_a_re, ssm_a_im, ssm_log_dt,
           ssm_b_re, ssm_b_im, ssm_c_re, ssm_c_im, ssm_d, w_glu, b_glu, g_ssm_out,
           g_attn_out, w_out, g_ffn, w_gate, w_up, w_down, g_final):
    depth = w_mod.shape[0]
    b, s, d = x.shape
    ssm_width = w_glu.shape[1]
    c_pad = jnp.zeros((SUBLANES, d), F32).at[:b].set(c.astype(F32))
    x = x.astype(F32)
    for l in range(depth):
        mod = _modulation(c_pad, w_mod[l], b_mod[l])[:b]
        sh1, sc1, gt1, sh2, sc2, gt2 = [m.reshape(b, 1, d) for m in jnp.split(mod, N_MOD, axis=-1)]
        proj = _projection(x, g_mix[l], sc1, sh1, w_in[l].astype(BF16))
        tables = _ssm_tables(ssm_a_re[l], ssm_a_im[l], ssm_log_dt[l], ssm_b_re[l], ssm_b_im[l],
                             ssm_c_re[l], ssm_c_im[l])
        y_ssm = _ssm(proj, tables, ssm_d[l], w_glu[l].astype(BF16), b_glu[l], g_ssm_out[l])
        y_att, (w_out_b, w_gate_b, w_up_b, w_down_b) = _attention(
            proj, positions, g_attn_out[l], ssm_width,
            [w_out[l].astype(F32), w_gate[l].astype(F32), w_up[l].astype(F32),
             w_down[l].astype(F32)])
        x = _out_projection(x, y_ssm, y_att, gt1, w_out_b)
        x = _ffn(x, g_ffn[l], sc2, sh2, gt2, g_final, w_gate_b, w_up_b, w_down_b,
                 final_norm=(l == depth - 1))
    return x
```

```python
import functools

import jax
import jax.numpy as jnp
from jax import lax
from jax.experimental import pallas as pl
from jax.experimental.pallas import tpu as pltpu

F32 = jnp.float32
BF16 = jnp.bfloat16

SSM_GROUP = 16
SSM_STATE = 64
HEAD_DIM = 128
DILATION_PAIRS = ((128, 1), (512, 4), (2048, 16))
Q_BLOCK = 128
ROPE_THETA = 10000.0
N_MOD = 6
EPS = 1e-6
NEG = -1e30

LANES = 128
SUBLANES = 8
BF16_ROWS = 16
VMEM_LIMIT = 60 * 1024 * 1024

MOD_TN = 1024
PROJ_TM = 512
PROJ_TN = 1024
FFN_TF = 512
FFN_SUB = 256
ATT_SKEW_SOFTMAX = 3
ATT_SKEW_PV = 6
SSM_TC = 512
PAIRS = 32
PITCH = PAIRS + 4
PAIR_VREGS = PAIRS // SUBLANES


def _params(sem):
    return pltpu.CompilerParams(dimension_semantics=sem, vmem_limit_bytes=VMEM_LIMIT)


def _const_spec(shape):
    nd = len(shape)
    return pl.BlockSpec(shape, lambda *_: (0,) * nd, pipeline_mode=pl.Buffered(1))


def _mod_kernel(c_ref, w_ref, b_ref, o_ref):
    c = c_ref[...]
    s = (c * jax.nn.sigmoid(c)).astype(BF16)
    o_ref[...] = jnp.dot(s, w_ref[...].astype(BF16), preferred_element_type=F32) + b_ref[...]


def _modulation(c_pad, w_mod, b_mod):
    rows, d = c_pad.shape
    n = w_mod.shape[1]
    return pl.pallas_call(
        _mod_kernel,
        grid=(n // MOD_TN,),
        in_specs=[
            pl.BlockSpec((rows, d), lambda j: (0, 0)),
            pl.BlockSpec((d, MOD_TN), lambda j: (0, j)),
            pl.BlockSpec((1, MOD_TN), lambda j: (0, j)),
        ],
        out_specs=pl.BlockSpec((rows, MOD_TN), lambda j: (0, j)),
        out_shape=jax.ShapeDtypeStruct((rows, n), F32),
        compiler_params=_params(("parallel",)),
        name="mod",
    )(c_pad, w_mod, b_mod.reshape(1, n))


def _modulated_norm(x, g, sc, sh):
    ms = jnp.mean(x * x, axis=-1, keepdims=True)
    return x * lax.rsqrt(ms + EPS) * (g * (1.0 + sc)) + sh


def _proj_kernel(x_ref, g_ref, sc_ref, sh_ref, w_ref, o_ref, h_ref):
    h_ref[...] = _modulated_norm(x_ref[0], g_ref[...], sc_ref[0], sh_ref[0]).astype(BF16)
    for n in range(w_ref.shape[1] // PROJ_TN):
        cols = slice(n * PROJ_TN, (n + 1) * PROJ_TN)
        o_ref[0, :, cols] = jnp.dot(h_ref[...], w_ref[:, cols],
                                    preferred_element_type=F32).astype(BF16)


def _projection(x, g, sc, sh, w_in):
    b, s, d = x.shape
    n = w_in.shape[1]
    row = lambda bi, i: (bi, i, 0)
    per_batch = lambda bi, i: (bi, 0, 0)
    return pl.pallas_call(
        _proj_kernel,
        grid=(b, s // PROJ_TM),
        in_specs=[
            pl.BlockSpec((1, PROJ_TM, d), row),
            _const_spec((1, d)),
            pl.BlockSpec((1, 1, d), per_batch),
            pl.BlockSpec((1, 1, d), per_batch),
            _const_spec((d, n)),
        ],
        out_specs=pl.BlockSpec((1, PROJ_TM, n), row),
        out_shape=jax.ShapeDtypeStruct((b, s, n), BF16),
        scratch_shapes=[pltpu.VMEM((PROJ_TM, d), BF16)],
        compiler_params=_params(("parallel", "parallel")),
        name="proj",
    )(x, g.reshape(1, d), sc, sh, w_in)


def _ssm_tables(a_re, a_im, log_dt, b_re, b_im, c_re, c_im):
    a_re = a_re.astype(F32); a_im = a_im.astype(F32)
    b_re = b_re.astype(F32); b_im = b_im.astype(F32)
    c_re = c_re.astype(F32); c_im = c_im.astype(F32)
    dt = jnp.exp(log_dt.astype(F32))[:, None]
    mag = jnp.exp(a_re * dt)
    abar_re = mag * jnp.cos(a_im * dt)
    abar_im = mag * jnp.sin(a_im * dt)
    den = a_re * a_re + a_im * a_im
    nr = abar_re - 1.0
    ni = abar_im
    f_re = (nr * a_re + ni * a_im) / den
    f_im = (ni * a_re - nr * a_im) / den
    bb_re = f_re[..., None] * b_re - f_im[..., None] * b_im
    bb_im = f_re[..., None] * b_im + f_im[..., None] * b_re

    g, p = a_re.shape
    cg = b_re.shape[-1]
    pairs = g // 2
    eye2 = jnp.eye(2, dtype=F32)

    def pair_in(t):
        t = t.transpose(0, 2, 1).reshape(pairs, 2, cg, p)
        return jnp.einsum("rkcp,kj->rkcjp", t, eye2).reshape(pairs, 2 * cg, 2 * p)

    blk = jnp.concatenate([pair_in(bb_re), pair_in(bb_im)], axis=-1)
    per_tile = LANES // (2 * cg)
    sel = jax.nn.one_hot(jnp.arange(pairs) % per_tile, per_tile, dtype=F32)
    bb = jnp.einsum("rcn,rq->rqcn", blk, sel).reshape(pairs, LANES, 4 * p)

    def pair_out(t):
        t = t.transpose(0, 2, 1).reshape(pairs, 2, p, cg)
        return jnp.einsum("rkpc,kj->rkpjc", t, eye2).reshape(pairs, 2 * p, 2 * cg)

    cblk = jnp.concatenate([pair_out(c_re), pair_out(-c_im)], axis=1)
    per_out = (2 * LANES) // (2 * cg)
    osel = jax.nn.one_hot(jnp.arange(pairs) % per_out, per_out, dtype=F32)
    cc = jnp.einsum("rkc,rq->rkqc", cblk, osel).reshape(pairs, 4 * p, 2 * LANES)

    def scan_rows(t):
        return t.reshape(SUBLANES, PAIR_VREGS, LANES).transpose(1, 0, 2)

    return bb.astype(BF16), cc.astype(BF16), scan_rows(abar_re), scan_rows(abar_im)


def _ssm_kernel(u_ref, bb_ref, ar_ref, ai_ref, cc_ref, d_ref, wglu_ref, bglu_ref, g_ref,
                o_ref, zre, zim, st, y_s):
    tc = u_ref.shape[1]

    @pl.when(pl.program_id(1) == 0)
    def _():
        st[...] = jnp.zeros_like(st)

    pairs_per_tile = PAIRS // (u_ref.shape[2] // LANES)
    for r in range(PAIRS):
        j = r // pairs_per_tile
        bu = jnp.dot(u_ref[0, :, j * LANES:(j + 1) * LANES], bb_ref[r],
                     preferred_element_type=F32)
        zre[pl.ds(r, tc, stride=PITCH), :] = bu[:, :LANES]
        zim[pl.ds(r, tc, stride=PITCH), :] = bu[:, LANES:]

    ar = [ar_ref[k] for k in range(PAIR_VREGS)]
    ai = [ai_ref[k] for k in range(PAIR_VREGS)]

    def step(t, carry):
        xr, xi = carry
        base = t * PITCH
        nxr, nxi = [], []
        for k in range(PAIR_VREGS):
            rows = pl.ds(base + k, SUBLANES, stride=PAIR_VREGS)
            nr = ar[k] * xr[k] - ai[k] * xi[k] + zre[rows, :]
            ni = ar[k] * xi[k] + ai[k] * xr[k] + zim[rows, :]
            zre[rows, :] = nr
            zim[rows, :] = ni
            nxr.append(nr)
            nxi.append(ni)
        return tuple(nxr), tuple(nxi)

    carry0 = (tuple(st[0, k] for k in range(PAIR_VREGS)),
              tuple(st[1, k] for k in range(PAIR_VREGS)))
    xr, xi = lax.fori_loop(0, tc, step, carry0, unroll=4)
    for k in range(PAIR_VREGS):
        st[0, k] = xr[k]
        st[1, k] = xi[k]

    out_tile = 2 * LANES
    pairs_per_out = PAIRS // (o_ref.shape[2] // out_tile)
    for j in range(o_ref.shape[2] // out_tile):
        acc = None
        for q in range(pairs_per_out):
            r = j * pairs_per_out + q
            lhs = jnp.concatenate([zre[pl.ds(r, tc, stride=PITCH), :].astype(BF16),
                                   zim[pl.ds(r, tc, stride=PITCH), :].astype(BF16)], axis=1)
            part = jnp.dot(lhs, cc_ref[r], preferred_element_type=F32)
            acc = part if acc is None else acc + part
        cols = slice(j * out_tile, (j + 1) * out_tile)
        y = acc + d_ref[:, cols] * u_ref[0, :, cols].astype(F32)
        y_s[:, cols] = jax.nn.gelu(y)

    vg = y_s[...]
    z = jnp.dot(vg.astype(BF16), wglu_ref[...], preferred_element_type=F32) + bglu_ref[...]
    out = vg * jax.nn.sigmoid(z)
    ms = jnp.mean(out * out, axis=-1, keepdims=True)
    o_ref[0] = (out * lax.rsqrt(ms + EPS) * g_ref[...]).astype(BF16)


def _ssm(proj, tables, d_skip, w_glu, b_glu, g_out):
    b, s, _ = proj.shape
    bb, cc, ar, ai = tables
    w = w_glu.shape[0]
    assert PAIRS * 2 * SSM_GROUP == w
    return pl.pallas_call(
        _ssm_kernel,
        grid=(b, s // SSM_TC),
        in_specs=[
            pl.BlockSpec((1, SSM_TC, w), lambda bi, t: (bi, t, 0)),
            _const_spec(bb.shape),
            _const_spec(ar.shape),
            _const_spec(ai.shape),
            _const_spec(cc.shape),
            _const_spec((1, w)),
            _const_spec((w, w)),
            _const_spec((1, w)),
            _const_spec((1, w)),
        ],
        out_specs=pl.BlockSpec((1, SSM_TC, w), lambda bi, t: (bi, t, 0)),
        out_shape=jax.ShapeDtypeStruct((b, s, w), BF16),
        scratch_shapes=[
            pltpu.VMEM((SSM_TC * PITCH, LANES), F32),
            pltpu.VMEM((SSM_TC * PITCH, LANES), F32),
            pltpu.VMEM((2, PAIR_VREGS, SUBLANES, LANES), F32),
            pltpu.VMEM((SSM_TC, w), F32),
        ],
        compiler_params=_params(("parallel", "arbitrary")),
        name="ssm",
    )(proj, bb, ar, ai, cc, d_skip.reshape(1, w), w_glu, b_glu.reshape(1, w),
      g_out.reshape(1, w))


def _attn_kernel(*refs, n_cast):
    (pos_ref, invf_ref, q_ref, k_ref, v_ref, g_ref), refs = refs[:6], refs[6:]
    cast_in, refs = refs[:n_cast], refs[n_cast:]
    o_ref, refs = refs[0], refs[1:]
    cast_out, refs = refs[:n_cast], refs[n_cast:]
    cos_s, sin_s, f_s, p_s, o_s, l_s, y_s = refs

    s = q_ref.shape[1]
    n_heads = y_s.shape[0]
    h = pl.program_id(1)
    half = HEAD_DIM // 2

    for src, dst in zip(cast_in, cast_out):
        dst[...] = src[...].astype(BF16)

    @pl.when(h == 0)
    def _():
        ang = pos_ref[0].astype(F32) * invf_ref[...]
        lane = lax.broadcasted_iota(jnp.int32, (1, HEAD_DIM), 1)
        cos_s[...] = jnp.cos(ang)
        sin_s[...] = jnp.sin(ang) * jnp.where(lane < half, -1.0, 1.0)

    def rope(t):
        return t * cos_s[...] + pltpu.roll(t, half, 1) * sin_s[...]

    f_s[0] = rope(q_ref[0].astype(F32)) * (HEAD_DIM ** -0.5)
    f_s[1] = rope(k_ref[0].astype(F32))
    f_s[2] = v_ref[0].astype(F32)

    for bi, (_, d) in enumerate(DILATION_PAIRS):
        seg = s // d
        for j in range(3):
            for r in range(d):
                src = f_s[j] if d == 1 else f_s[j, pl.ds(r, seg, stride=d), :]
                p_s[bi, j, r * seg:(r + 1) * seg, :] = src.astype(BF16)

    def banded_mask(n_keys, steps):
        row = lax.broadcasted_iota(jnp.int32, (Q_BLOCK, n_keys), 0)
        col = lax.broadcasted_iota(jnp.int32, (Q_BLOCK, n_keys), 1)
        dist = row + (n_keys - Q_BLOCK) - col
        return (dist >= 0) & (dist <= steps)

    jobs = []
    for bi, (window, d) in enumerate(DILATION_PAIRS):
        steps = window // d
        span = Q_BLOCK * d
        nb = s // span
        masks = {nk: banded_mask(nk, steps) for nk in (Q_BLOCK, 2 * Q_BLOCK)}
        for i in range(s // Q_BLOCK):
            r, n = divmod(i, nb)
            lo = (i - 1) * Q_BLOCK if n > 0 else i * Q_BLOCK
            start = r + n * span
            dst = pl.ds(start, Q_BLOCK) if d == 1 else pl.ds(start, Q_BLOCK, stride=d)
            jobs.append((bi, i * Q_BLOCK, lo, (i + 1) * Q_BLOCK, masks, dst))

    def scores(job):
        bi, q0, lo, hi, masks, _ = job
        sc = lax.dot_general(p_s[bi, 0, q0:hi, :], p_s[bi, 1, lo:hi, :],
                             (((1,), (1,)), ((), ())), preferred_element_type=F32)
        return jnp.where(masks[hi - lo], sc, NEG)

    def softmax(sc):
        m = jnp.max(sc, axis=-1, keepdims=True)
        p = jnp.exp(sc - m)
        l = jnp.sum(p, axis=-1, keepdims=True)
        return p.astype(BF16), m, l

    def weighted(job, p, m, l):
        bi, _, lo, hi, _, dst = job
        pv = jnp.dot(p, p_s[bi, 2, lo:hi, :], preferred_element_type=F32)
        o_s[bi, dst, :] = pv * (1.0 / l)
        l_s[bi, dst, :] = jnp.broadcast_to(m + jnp.log(l), (Q_BLOCK, HEAD_DIM))

    sc_of, sm_of = {}, {}
    for t in range(len(jobs) + ATT_SKEW_PV):
        if t < len(jobs):
            sc_of[t] = scores(jobs[t])
        if 0 <= t - ATT_SKEW_SOFTMAX < len(jobs):
            sm_of[t - ATT_SKEW_SOFTMAX] = softmax(sc_of.pop(t - ATT_SKEW_SOFTMAX))
        if 0 <= t - ATT_SKEW_PV < len(jobs):
            weighted(jobs[t - ATT_SKEW_PV], *sm_of.pop(t - ATT_SKEW_PV))

    lses = [l_s[bi] for bi in range(len(DILATION_PAIRS))]
    top = functools.reduce(jnp.maximum, lses)
    es = [jnp.exp(x - top) for x in lses]
    num = sum(e * o_s[bi] for bi, e in enumerate(es))
    y_s[h] = num / sum(es)

    @pl.when(h == n_heads - 1)
    def _():
        ss = sum(jnp.sum(y_s[i] * y_s[i], axis=-1, keepdims=True) for i in range(n_heads))
        inv = lax.rsqrt(ss / (n_heads * HEAD_DIM) + EPS)
        for i in range(n_heads):
            cols = slice(i * HEAD_DIM, (i + 1) * HEAD_DIM)
            o_ref[0, :, cols] = (y_s[i] * inv * g_ref[:, cols]).astype(BF16)


def _attention(proj, positions, g_out, ssm_width, weights):
    b, s, _ = proj.shape
    w = g_out.shape[0]
    n_heads = w // HEAD_DIM
    half = HEAD_DIM // 2
    inv_freq = ROPE_THETA ** (-jnp.arange(half, dtype=F32) / half)
    inv_freq = jnp.concatenate([inv_freq, inv_freq]).reshape(1, HEAD_DIM)
    first = ssm_width // HEAD_DIM
    steps = b * n_heads

    def head_cols(offset):
        return pl.BlockSpec((1, s, HEAD_DIM), lambda bi, h: (bi, 0, first + offset * n_heads + h))

    def slab(wt):
        rows = wt.shape[0] // steps
        assert rows * steps == wt.shape[0] and rows % BF16_ROWS == 0
        return pl.BlockSpec((rows, wt.shape[1]), lambda bi, h: (bi * n_heads + h, 0))

    n_br = len(DILATION_PAIRS)
    outs = pl.pallas_call(
        functools.partial(_attn_kernel, n_cast=len(weights)),
        grid=(b, n_heads),
        in_specs=[
            pl.BlockSpec((1, s, 1), lambda bi, h: (bi, 0, 0)),
            _const_spec((1, HEAD_DIM)),
            head_cols(0), head_cols(1), head_cols(2),
            _const_spec((1, w)),
        ] + [slab(wt) for wt in weights],
        out_specs=[pl.BlockSpec((1, s, w), lambda bi, h: (bi, 0, 0), pipeline_mode=pl.Buffered(1))]
        + [slab(wt) for wt in weights],
        out_shape=[jax.ShapeDtypeStruct((b, s, w), BF16)]
        + [jax.ShapeDtypeStruct(wt.shape, BF16) for wt in weights],
        scratch_shapes=[
            pltpu.VMEM((s, HEAD_DIM), F32),
            pltpu.VMEM((s, HEAD_DIM), F32),
            pltpu.VMEM((3, s, HEAD_DIM), F32),
            pltpu.VMEM((n_br, 3, s, HEAD_DIM), BF16),
            pltpu.VMEM((n_br, s, HEAD_DIM), F32),
            pltpu.VMEM((n_br, s, HEAD_DIM), F32),
            pltpu.VMEM((n_heads, s, HEAD_DIM), F32),
        ],
        compiler_params=_params(("parallel", "arbitrary")),
        name="attn",
    )(positions.reshape(b, s, 1), inv_freq, proj, proj, proj, g_out.reshape(1, w), *weights)
    return outs[0], outs[1:]


def _outproj_kernel(x_ref, ys_ref, ya_ref, gt_ref, w_ref, o_ref):
    ws = ys_ref.shape[2]
    acc = jnp.dot(ys_ref[0], w_ref[:ws, :], preferred_element_type=F32)
    acc = acc + jnp.dot(ya_ref[0], w_ref[ws:, :], preferred_element_type=F32)
    o_ref[0] = x_ref[0] + gt_ref[0] * acc


def _out_projection(x, y_ssm, y_att, gt, w_out):
    b, s, d = x.shape
    row = lambda bi, i: (bi, i, 0)
    return pl.pallas_call(
        _outproj_kernel,
        grid=(b, s // PROJ_TM),
        in_specs=[
            pl.BlockSpec((1, PROJ_TM, d), row),
            pl.BlockSpec((1, PROJ_TM, y_ssm.shape[2]), row),
            pl.BlockSpec((1, PROJ_TM, y_att.shape[2]), row),
            pl.BlockSpec((1, 1, d), lambda bi, i: (bi, 0, 0)),
            _const_spec(w_out.shape),
        ],
        out_specs=pl.BlockSpec((1, PROJ_TM, d), row),
        out_shape=jax.ShapeDtypeStruct((b, s, d), F32),
        compiler_params=_params(("parallel", "parallel")),
        name="outproj",
    )(x, y_ssm, y_att, gt, w_out)


def _ffn_kernel(x_ref, g_ref, sc_ref, sh_ref, gt_ref, gf_ref, wg_ref, wu_ref, wd_ref,
                o_ref, h_ref, acc_ref, *, final_norm):
    j = pl.program_id(2)

    @pl.when(j == 0)
    def _():
        h_ref[...] = _modulated_norm(x_ref[0], g_ref[...], sc_ref[0], sh_ref[0]).astype(BF16)
        acc_ref[...] = jnp.zeros_like(acc_ref)

    h = h_ref[...]
    acts = []
    for c in range(wg_ref.shape[1] // FFN_SUB):
        cols = slice(c * FFN_SUB, (c + 1) * FFN_SUB)
        gate = jnp.dot(h, wg_ref[:, cols], preferred_element_type=F32)
        up = jnp.dot(h, wu_ref[:, cols], preferred_element_type=F32)
        acts.append((gate * jax.nn.sigmoid(gate) * up).astype(BF16))
    down = None
    for c, act in enumerate(acts):
        part = jnp.dot(act, wd_ref[c * FFN_SUB:(c + 1) * FFN_SUB, :], preferred_element_type=F32)
        down = part if down is None else down + part
    acc_ref[...] += down

    @pl.when(j == pl.num_programs(2) - 1)
    def _():
        y = x_ref[0] + gt_ref[0] * acc_ref[...]
        if final_norm:
            ms = jnp.mean(y * y, axis=-1, keepdims=True)
            y = y * lax.rsqrt(ms + EPS) * gf_ref[...]
        o_ref[0] = y


def _ffn(x, g, sc, sh, gt, g_final, w_gate, w_up, w_down, final_norm):
    b, s, d = x.shape
    dff = w_gate.shape[1]
    row = lambda bi, i, j: (bi, i, 0)
    per_batch = lambda bi, i, j: (bi, 0, 0)
    return pl.pallas_call(
        functools.partial(_ffn_kernel, final_norm=final_norm),
        grid=(b, s // PROJ_TM, dff // FFN_TF),
        in_specs=[
            pl.BlockSpec((1, PROJ_TM, d), row),
            _const_spec((1, d)),
            pl.BlockSpec((1, 1, d), per_batch),
            pl.BlockSpec((1, 1, d), per_batch),
            pl.BlockSpec((1, 1, d), per_batch),
            _const_spec((1, d)),
            pl.BlockSpec((d, FFN_TF), lambda bi, i, j: (0, j)),
            pl.BlockSpec((d, FFN_TF), lambda bi, i, j: (0, j)),
            pl.BlockSpec((FFN_TF, d), lambda bi, i, j: (j, 0)),
        ],
        out_specs=pl.BlockSpec((1, PROJ_TM, d), row),
        out_shape=jax.ShapeDtypeStruct((b, s, d), F32),
        scratch_shapes=[pltpu.VMEM((PROJ_TM, d), BF16), pltpu.VMEM((PROJ_TM, d), F32)],
        compiler_params=_params(("parallel", "parallel", "arbitrary")),
        name="ffn",
    )(x, g.reshape(1, d), sc, sh, gt, g_final.reshape(1, d), w_gate, w_up, w_down)


def kernel(x, c, positions, w_mod, b_mod, g_mix, w_in, ssm_a_re, ssm_a_im, ssm_log_dt,
           ssm_b_re, ssm_b_im, ssm_c_re, ssm_c_im, ssm_d, w_glu, b_glu, g_ssm_out,
           g_attn_out, w_out, g_ffn, w_gate, w_up, w_down, g_final):
    depth = w_mod.shape[0]
    b, s, d = x.shape
    ssm_width = w_glu.shape[1]
    c_pad = jnp.zeros((SUBLANES, d), F32).at[:b].set(c.astype(F32))
    x = x.astype(F32)
    for l in range(depth):
        mod = _modulation(c_pad, w_mod[l], b_mod[l])[:b]
        sh1, sc1, gt1, sh2, sc2, gt2 = [m.reshape(b, 1, d) for m in jnp.split(mod, N_MOD, axis=-1)]
        proj = _projection(x, g_mix[l], sc1, sh1, w_in[l].astype(BF16))
        tables = _ssm_tables(ssm_a_re[l], ssm_a_im[l], ssm_log_dt[l], ssm_b_re[l], ssm_b_im[l],
                             ssm_c_re[l], ssm_c_im[l])
        y_ssm = _ssm(proj, tables, ssm_d[l], w_glu[l].astype(BF16), b_glu[l], g_ssm_out[l])
        y_att, (w_out_b, w_gate_b, w_up_b, w_down_b) = _attention(
            proj, positions, g_attn_out[l], ssm_width,
            [w_out[l].astype(F32), w_gate[l].astype(F32), w_up[l].astype(F32),
             w_down[l].astype(F32)])
        x = _out_projection(x, y_ssm, y_att, gt1, w_out_b)
        x = _ffn(x, g_ffn[l], sc2, sh2, gt2, g_final, w_gate_b, w_up_b, w_down_b,
                 final_norm=(l == depth - 1))
    return x
```

```python
import functools

import jax
import jax.numpy as jnp
from jax import lax
from jax.experimental import pallas as pl
from jax.experimental.pallas import tpu as pltpu

F32 = jnp.float32
BF16 = jnp.bfloat16

SSM_GROUP = 16
SSM_STATE = 64
HEAD_DIM = 128
DILATION_PAIRS = ((128, 1), (512, 4), (2048, 16))
Q_BLOCK = 128
ROPE_THETA = 10000.0
N_MOD = 6
EPS = 1e-6
NEG = -1e30

LANES = 128
SUBLANES = 8
BF16_ROWS = 16
VMEM_LIMIT = 60 * 1024 * 1024

MOD_TN = 1024
PROJ_TM = 512
PROJ_TN = 1024
FFN_TF = 512
FFN_SUB = 256
ATT_SKEW_SOFTMAX = 3
ATT_SKEW_PV = 6
SSM_TC = 512
PAIRS = 32
PITCH = PAIRS + 4
PAIR_VREGS = PAIRS // SUBLANES


def _params(sem):
    return pltpu.CompilerParams(dimension_semantics=sem, vmem_limit_bytes=VMEM_LIMIT)


def _const_spec(shape):
    nd = len(shape)
    return pl.BlockSpec(shape, lambda *_: (0,) * nd, pipeline_mode=pl.Buffered(1))


def _mod_kernel(c_ref, w_ref, b_ref, o_ref):
    c = c_ref[...]
    s = (c * jax.nn.sigmoid(c)).astype(BF16)
    o_ref[...] = jnp.dot(s, w_ref[...].astype(BF16), preferred_element_type=F32) + b_ref[...]


def _modulation(c_pad, w_mod, b_mod):
    rows, d = c_pad.shape
    n = w_mod.shape[1]
    return pl.pallas_call(
        _mod_kernel,
        grid=(n // MOD_TN,),
        in_specs=[
            pl.BlockSpec((rows, d), lambda j: (0, 0)),
            pl.BlockSpec((d, MOD_TN), lambda j: (0, j)),
            pl.BlockSpec((1, MOD_TN), lambda j: (0, j)),
        ],
        out_specs=pl.BlockSpec((rows, MOD_TN), lambda j: (0, j)),
        out_shape=jax.ShapeDtypeStruct((rows, n), F32),
        compiler_params=_params(("parallel",)),
        name="mod",
    )(c_pad, w_mod, b_mod.reshape(1, n))


def _modulated_norm(x, g, sc, sh):
    ms = jnp.mean(x * x, axis=-1, keepdims=True)
    return x * lax.rsqrt(ms + EPS) * (g * (1.0 + sc)) + sh


def _proj_kernel(x_ref, g_ref, sc_ref, sh_ref, w_ref, pos_ref, invf_ref,
                 u_ref, *rest, ssm_width):
    n_br = len(DILATION_PAIRS)
    qkv_refs, (h_ref, stage) = rest[:n_br], rest[n_br:]
    tm = x_ref.shape[1]
    n_heads = stage.shape[1]
    attn_width = n_heads * HEAD_DIM
    half = HEAD_DIM // 2

    h_ref[...] = _modulated_norm(x_ref[0], g_ref[...], sc_ref[0], sh_ref[0]).astype(BF16)
    u_ref[0] = jnp.dot(h_ref[...], w_ref[:, :ssm_width], preferred_element_type=F32).astype(BF16)

    ang = pos_ref[0].astype(F32) * invf_ref[...]
    lane = lax.broadcasted_iota(jnp.int32, (1, HEAD_DIM), 1)
    cos = jnp.cos(ang)
    sin = jnp.sin(ang) * jnp.where(lane < half, -1.0, 1.0)

    for part in range(3):
        first = ssm_width + part * attn_width
        res = jnp.dot(h_ref[...], w_ref[:, first:first + attn_width], preferred_element_type=F32)
        for hd in range(n_heads):
            t = res[:, hd * HEAD_DIM:(hd + 1) * HEAD_DIM]
            if part < 2:
                t = t * cos + pltpu.roll(t, half, 1) * sin
            if part == 0:
                t = t * (HEAD_DIM ** -0.5)
            stage[0, hd] = t
        prev_d = 1
        for bi, (_, d) in enumerate(DILATION_PAIRS):
            ratio = d // prev_d
            assert ratio * prev_d == d and tm % (d * BF16_ROWS) == 0
            seg, prev_seg = tm // d, tm // prev_d
            for hd in range(n_heads):
                cols = slice(part * attn_width + hd * HEAD_DIM,
                             part * attn_width + (hd + 1) * HEAD_DIM)
                if bi == 0:
                    qkv_refs[0][0, :, cols] = stage[0, hd].astype(BF16)
                    continue
                for r in range(d):
                    r_prev, r_sub = r % prev_d, r // prev_d
                    src = stage[bi - 1, hd, pl.ds(r_prev * prev_seg + r_sub, seg, stride=ratio), :]
                    if bi + 1 < n_br:
                        stage[bi, hd, r * seg:(r + 1) * seg, :] = src
                    qkv_refs[bi][0, r, :, cols] = src.astype(BF16)
            prev_d = d


def _projection(x, g, sc, sh, w_in, positions, ssm_width):
    b, s, d = x.shape
    n = w_in.shape[1]
    attn3 = n - ssm_width
    n_heads = attn3 // (3 * HEAD_DIM)
    half = HEAD_DIM // 2
    inv_freq = ROPE_THETA ** (-jnp.arange(half, dtype=F32) / half)
    inv_freq = jnp.concatenate([inv_freq, inv_freq]).reshape(1, HEAD_DIM)
    row = lambda bi, i: (bi, i, 0)
    per_batch = lambda bi, i: (bi, 0, 0)
    out_specs = [pl.BlockSpec((1, PROJ_TM, ssm_width), row)]
    out_shape = [jax.ShapeDtypeStruct((b, s, ssm_width), BF16)]
    for _, dil in DILATION_PAIRS:
        if dil == 1:
            out_specs.append(pl.BlockSpec((1, PROJ_TM, attn3), row))
            out_shape.append(jax.ShapeDtypeStruct((b, s, attn3), BF16))
        else:
            out_specs.append(pl.BlockSpec((1, dil, PROJ_TM // dil, attn3),
                                          lambda bi, i: (bi, 0, i, 0)))
            out_shape.append(jax.ShapeDtypeStruct((b, dil, s // dil, attn3), BF16))
    n_stage = len(DILATION_PAIRS) - 1
    return pl.pallas_call(
        functools.partial(_proj_kernel, ssm_width=ssm_width),
        grid=(b, s // PROJ_TM),
        in_specs=[
            pl.BlockSpec((1, PROJ_TM, d), row),
            _const_spec((1, d)),
            pl.BlockSpec((1, 1, d), per_batch),
            pl.BlockSpec((1, 1, d), per_batch),
            _const_spec((d, n)),
            pl.BlockSpec((1, PROJ_TM, 1), row),
            _const_spec((1, HEAD_DIM)),
        ],
        out_specs=out_specs,
        out_shape=out_shape,
        scratch_shapes=[pltpu.VMEM((PROJ_TM, d), BF16),
                        pltpu.VMEM((n_stage, n_heads, PROJ_TM, HEAD_DIM), F32)],
        compiler_params=_params(("parallel", "parallel")),
        name="proj",
    )(x, g.reshape(1, d), sc, sh, w_in, positions.reshape(b, s, 1), inv_freq)


def _ssm_tables(a_re, a_im, log_dt, b_re, b_im, c_re, c_im):
    a_re = a_re.astype(F32); a_im = a_im.astype(F32)
    b_re = b_re.astype(F32); b_im = b_im.astype(F32)
    c_re = c_re.astype(F32); c_im = c_im.astype(F32)
    dt = jnp.exp(log_dt.astype(F32))[:, None]
    mag = jnp.exp(a_re * dt)
    abar_re = mag * jnp.cos(a_im * dt)
    abar_im = mag * jnp.sin(a_im * dt)
    den = a_re * a_re + a_im * a_im
    nr = abar_re - 1.0
    ni = abar_im
    f_re = (nr * a_re + ni * a_im) / den
    f_im = (ni * a_re - nr * a_im) / den
    bb_re = f_re[..., None] * b_re - f_im[..., None] * b_im
    bb_im = f_re[..., None] * b_im + f_im[..., None] * b_re

    g, p = a_re.shape
    cg = b_re.shape[-1]
    pairs = g // 2
    eye2 = jnp.eye(2, dtype=F32)

    def pair_in(t):
        t = t.transpose(0, 2, 1).reshape(pairs, 2, cg, p)
        return jnp.einsum("rkcp,kj->rkcjp", t, eye2).reshape(pairs, 2 * cg, 2 * p)

    blk = jnp.concatenate([pair_in(bb_re), pair_in(bb_im)], axis=-1)
    per_tile = LANES // (2 * cg)
    sel = jax.nn.one_hot(jnp.arange(pairs) % per_tile, per_tile, dtype=F32)
    bb = jnp.einsum("rcn,rq->rqcn", blk, sel).reshape(pairs, LANES, 4 * p)

    def pair_out(t):
        t = t.transpose(0, 2, 1).reshape(pairs, 2, p, cg)
        return jnp.einsum("rkpc,kj->rkpjc", t, eye2).reshape(pairs, 2 * p, 2 * cg)

    cblk = jnp.concatenate([pair_out(c_re), pair_out(-c_im)], axis=1)
    per_out = (2 * LANES) // (2 * cg)
    osel = jax.nn.one_hot(jnp.arange(pairs) % per_out, per_out, dtype=F32)
    cc = jnp.einsum("rkc,rq->rkqc", cblk, osel).reshape(pairs, 4 * p, 2 * LANES)

    def scan_rows(t):
        return t.reshape(SUBLANES, PAIR_VREGS, LANES).transpose(1, 0, 2)

    return bb.astype(BF16), cc.astype(BF16), scan_rows(abar_re), scan_rows(abar_im)


def _ssm_kernel(u_ref, bb_ref, ar_ref, ai_ref, cc_ref, d_ref, wglu_ref, bglu_ref, g_ref,
                o_ref, zre, zim, st, y_s):
    tc = u_ref.shape[1]

    @pl.when(pl.program_id(1) == 0)
    def _():
        st[...] = jnp.zeros_like(st)

    pairs_per_tile = PAIRS // (u_ref.shape[2] // LANES)
    for r in range(PAIRS):
        j = r // pairs_per_tile
        bu = jnp.dot(u_ref[0, :, j * LANES:(j + 1) * LANES], bb_ref[r],
                     preferred_element_type=F32)
        zre[pl.ds(r, tc, stride=PITCH), :] = bu[:, :LANES]
        zim[pl.ds(r, tc, stride=PITCH), :] = bu[:, LANES:]

    ar = [ar_ref[k] for k in range(PAIR_VREGS)]
    ai = [ai_ref[k] for k in range(PAIR_VREGS)]

    def step(t, carry):
        xr, xi = carry
        base = t * PITCH
        nxr, nxi = [], []
        for k in range(PAIR_VREGS):
            rows = pl.ds(base + k, SUBLANES, stride=PAIR_VREGS)
            nr = ar[k] * xr[k] - ai[k] * xi[k] + zre[rows, :]
            ni = ar[k] * xi[k] + ai[k] * xr[k] + zim[rows, :]
            zre[rows, :] = nr
            zim[rows, :] = ni
            nxr.append(nr)
            nxi.append(ni)
        return tuple(nxr), tuple(nxi)

    carry0 = (tuple(st[0, k] for k in range(PAIR_VREGS)),
              tuple(st[1, k] for k in range(PAIR_VREGS)))
    xr, xi = lax.fori_loop(0, tc, step, carry0, unroll=4)
    for k in range(PAIR_VREGS):
        st[0, k] = xr[k]
        st[1, k] = xi[k]

    out_tile = 2 * LANES
    pairs_per_out = PAIRS // (o_ref.shape[2] // out_tile)
    for j in range(o_ref.shape[2] // out_tile):
        acc = None
        for q in range(pairs_per_out):
            r = j * pairs_per_out + q
            lhs = jnp.concatenate([zre[pl.ds(r, tc, stride=PITCH), :].astype(BF16),
                                   zim[pl.ds(r, tc, stride=PITCH), :].astype(BF16)], axis=1)
            part = jnp.dot(lhs, cc_ref[r], preferred_element_type=F32)
            acc = part if acc is None else acc + part
        cols = slice(j * out_tile, (j + 1) * out_tile)
        y = acc + d_ref[:, cols] * u_ref[0, :, cols].astype(F32)
        y_s[:, cols] = jax.nn.gelu(y)

    vg = y_s[...]
    z = jnp.dot(vg.astype(BF16), wglu_ref[...], preferred_element_type=F32) + bglu_ref[...]
    out = vg * jax.nn.sigmoid(z)
    ms = jnp.mean(out * out, axis=-1, keepdims=True)
    o_ref[0] = (out * lax.rsqrt(ms + EPS) * g_ref[...]).astype(BF16)


def _ssm(proj, tables, d_skip, w_glu, b_glu, g_out):
    b, s, _ = proj.shape
    bb, cc, ar, ai = tables
    w = w_glu.shape[0]
    assert PAIRS * 2 * SSM_GROUP == w
    return pl.pallas_call(
        _ssm_kernel,
        grid=(b, s // SSM_TC),
        in_specs=[
            pl.BlockSpec((1, SSM_TC, w), lambda bi, t: (bi, t, 0)),
            _const_spec(bb.shape),
            _const_spec(ar.shape),
            _const_spec(ai.shape),
            _const_spec(cc.shape),
            _const_spec((1, w)),
            _const_spec((w, w)),
            _const_spec((1, w)),
            _const_spec((1, w)),
        ],
        out_specs=pl.BlockSpec((1, SSM_TC, w), lambda bi, t: (bi, t, 0)),
        out_shape=jax.ShapeDtypeStruct((b, s, w), BF16),
        scratch_shapes=[
            pltpu.VMEM((SSM_TC * PITCH, LANES), F32),
            pltpu.VMEM((SSM_TC * PITCH, LANES), F32),
            pltpu.VMEM((2, PAIR_VREGS, SUBLANES, LANES), F32),
            pltpu.VMEM((SSM_TC, w), F32),
        ],
        compiler_params=_params(("parallel", "arbitrary")),
        name="ssm",
    )(proj, bb, ar, ai, cc, d_skip.reshape(1, w), w_glu, b_glu.reshape(1, w),
      g_out.reshape(1, w))


def _attn_kernel(*refs, n_cast):
    n_br = len(DILATION_PAIRS)
    qkv, refs = refs[:3 * n_br], refs[3 * n_br:]
    qkv = [qkv[3 * bi:3 * bi + 3] for bi in range(n_br)]
    g_ref, refs = refs[0], refs[1:]
    cast_in, refs = refs[:n_cast], refs[n_cast:]
    o_ref, refs = refs[0], refs[1:]
    cast_out, refs = refs[:n_cast], refs[n_cast:]
    o_s, l_s, y_s = refs

    s = o_ref.shape[1]
    n_heads = y_s.shape[0]
    h = pl.program_id(1)

    for src, dst in zip(cast_in, cast_out):
        dst[...] = src[...].astype(BF16)

    def rows(bi, j, r, lo, hi):
        ref = qkv[bi][j]
        return ref[0, lo:hi, :] if len(ref.shape) == 3 else ref[0, r, lo:hi, :]

    def banded_mask(n_keys, steps):
        row = lax.broadcasted_iota(jnp.int32, (Q_BLOCK, n_keys), 0)
        col = lax.broadcasted_iota(jnp.int32, (Q_BLOCK, n_keys), 1)
        dist = row + (n_keys - Q_BLOCK) - col
        return (dist >= 0) & (dist <= steps)

    jobs = []
    for bi, (window, d) in enumerate(DILATION_PAIRS):
        steps = window // d
        span = Q_BLOCK * d
        nb = s // span
        masks = {nk: banded_mask(nk, steps) for nk in (Q_BLOCK, 2 * Q_BLOCK)}
        for r in range(d):
            for n in range(nb):
                lo = (n - 1) * Q_BLOCK if n > 0 else 0
                start = r + n * span
                dst = pl.ds(start, Q_BLOCK) if d == 1 else pl.ds(start, Q_BLOCK, stride=d)
                jobs.append((bi, r, n * Q_BLOCK, lo, (n + 1) * Q_BLOCK, masks, dst))

    def scores(job):
        bi, r, q0, lo, hi, masks, _ = job
        sc = lax.dot_general(rows(bi, 0, r, q0, hi), rows(bi, 1, r, lo, hi),
                             (((1,), (1,)), ((), ())), preferred_element_type=F32)
        return jnp.where(masks[hi - lo], sc, NEG)

    def softmax(sc):
        m = jnp.max(sc, axis=-1, keepdims=True)
        p = jnp.exp(sc - m)
        l = jnp.sum(p, axis=-1, keepdims=True)
        return p.astype(BF16), m, l

    def weighted(job, p, m, l):
        bi, r, _, lo, hi, _, dst = job
        pv = jnp.dot(p, rows(bi, 2, r, lo, hi), preferred_element_type=F32)
        o_s[bi, dst, :] = pv * (1.0 / l)
        l_s[bi, dst, :] = jnp.broadcast_to(m + jnp.log(l), (Q_BLOCK, HEAD_DIM))

    sc_of, sm_of = {}, {}
    for t in range(len(jobs) + ATT_SKEW_PV):
        if t < len(jobs):
            sc_of[t] = scores(jobs[t])
        if 0 <= t - ATT_SKEW_SOFTMAX < len(jobs):
            sm_of[t - ATT_SKEW_SOFTMAX] = softmax(sc_of.pop(t - ATT_SKEW_SOFTMAX))
        if 0 <= t - ATT_SKEW_PV < len(jobs):
            weighted(jobs[t - ATT_SKEW_PV], *sm_of.pop(t - ATT_SKEW_PV))

    lses = [l_s[bi] for bi in range(len(DILATION_PAIRS))]
    top = functools.reduce(jnp.maximum, lses)
    es = [jnp.exp(x - top) for x in lses]
    num = sum(e * o_s[bi] for bi, e in enumerate(es))
    y_s[h] = num / sum(es)

    @pl.when(h == n_heads - 1)
    def _():
        ss = sum(jnp.sum(y_s[i] * y_s[i], axis=-1, keepdims=True) for i in range(n_heads))
        inv = lax.rsqrt(ss / (n_heads * HEAD_DIM) + EPS)
        for i in range(n_heads):
            cols = slice(i * HEAD_DIM, (i + 1) * HEAD_DIM)
            o_ref[0, :, cols] = (y_s[i] * inv * g_ref[:, cols]).astype(BF16)


def _attention(qkv_by_branch, g_out, weights):
    b, s = qkv_by_branch[0].shape[:2]
    w = g_out.shape[0]
    n_heads = w // HEAD_DIM
    steps = b * n_heads

    def head_cols(arr, j):
        if arr.ndim == 3:
            return pl.BlockSpec((1, s, HEAD_DIM), lambda bi, h: (bi, 0, j * n_heads + h))
        return pl.BlockSpec((1,) + arr.shape[1:3] + (HEAD_DIM,),
                            lambda bi, h: (bi, 0, 0, j * n_heads + h))

    qkv_specs = [head_cols(arr, j) for arr in qkv_by_branch for j in range(3)]
    qkv_args = [arr for arr in qkv_by_branch for _ in range(3)]

    def slab(wt):
        rows = wt.shape[0] // steps
        assert rows * steps == wt.shape[0] and rows % BF16_ROWS == 0
        return pl.BlockSpec((rows, wt.shape[1]), lambda bi, h: (bi * n_heads + h, 0))

    n_br = len(DILATION_PAIRS)
    outs = pl.pallas_call(
        functools.partial(_attn_kernel, n_cast=len(weights)),
        grid=(b, n_heads),
        in_specs=qkv_specs + [_const_spec((1, w))] + [slab(wt) for wt in weights],
        out_specs=[pl.BlockSpec((1, s, w), lambda bi, h: (bi, 0, 0), pipeline_mode=pl.Buffered(1))]
        + [slab(wt) for wt in weights],
        out_shape=[jax.ShapeDtypeStruct((b, s, w), BF16)]
        + [jax.ShapeDtypeStruct(wt.shape, BF16) for wt in weights],
        scratch_shapes=[
            pltpu.VMEM((n_br, s, HEAD_DIM), F32),
            pltpu.VMEM((n_br, s, HEAD_DIM), F32),
            pltpu.VMEM((n_heads, s, HEAD_DIM), F32),
        ],
        compiler_params=_params(("parallel", "arbitrary")),
        name="attn",
    )(*qkv_args, g_out.reshape(1, w), *weights)
    return outs[0], outs[1:]


def _outproj_kernel(x_ref, ys_ref, ya_ref, gt_ref, w_ref, o_ref):
    ws = ys_ref.shape[2]
    acc = jnp.dot(ys_ref[0], w_ref[:ws, :], preferred_element_type=F32)
    acc = acc + jnp.dot(ya_ref[0], w_ref[ws:, :], preferred_element_type=F32)
    o_ref[0] = x_ref[0] + gt_ref[0] * acc


def _out_projection(x, y_ssm, y_att, gt, w_out):
    b, s, d = x.shape
    row = lambda bi, i: (bi, i, 0)
    return pl.pallas_call(
        _outproj_kernel,
        grid=(b, s // PROJ_TM),
        in_specs=[
            pl.BlockSpec((1, PROJ_TM, d), row),
            pl.BlockSpec((1, PROJ_TM, y_ssm.shape[2]), row),
            pl.BlockSpec((1, PROJ_TM, y_att.shape[2]), row),
            pl.BlockSpec((1, 1, d), lambda bi, i: (bi, 0, 0)),
            _const_spec(w_out.shape),
        ],
        out_specs=pl.BlockSpec((1, PROJ_TM, d), row),
        out_shape=jax.ShapeDtypeStruct((b, s, d), F32),
        compiler_params=_params(("parallel", "parallel")),
        name="outproj",
    )(x, y_ssm, y_att, gt, w_out)


def _ffn_kernel(x_ref, g_ref, sc_ref, sh_ref, gt_ref, gf_ref, wg_ref, wu_ref, wd_ref,
                o_ref, h_ref, acc_ref, *, final_norm):
    j = pl.program_id(2)

    @pl.when(j == 0)
    def _():
        h_ref[...] = _modulated_norm(x_ref[0], g_ref[...], sc_ref[0], sh_ref[0]).astype(BF16)
        acc_ref[...] = jnp.zeros_like(acc_ref)

    h = h_ref[...]
    acts = []
    for c in range(wg_ref.shape[1] // FFN_SUB):
        cols = slice(c * FFN_SUB, (c + 1) * FFN_SUB)
        gate = jnp.dot(h, wg_ref[:, cols], preferred_element_type=F32)
        up = jnp.dot(h, wu_ref[:, cols], preferred_element_type=F32)
        acts.append((gate * jax.nn.sigmoid(gate) * up).astype(BF16))
    down = None
    for c, act in enumerate(acts):
        part = jnp.dot(act, wd_ref[c * FFN_SUB:(c + 1) * FFN_SUB, :], preferred_element_type=F32)
        down = part if down is None else down + part
    acc_ref[...] += down

    @pl.when(j == pl.num_programs(2) - 1)
    def _():
        y = x_ref[0] + gt_ref[0] * acc_ref[...]
        if final_norm:
            ms = jnp.mean(y * y, axis=-1, keepdims=True)
            y = y * lax.rsqrt(ms + EPS) * gf_ref[...]
        o_ref[0] = y


def _ffn(x, g, sc, sh, gt, g_final, w_gate, w_up, w_down, final_norm):
    b, s, d = x.shape
    dff = w_gate.shape[1]
    row = lambda bi, i, j: (bi, i, 0)
    per_batch = lambda bi, i, j: (bi, 0, 0)
    return pl.pallas_call(
        functools.partial(_ffn_kernel, final_norm=final_norm),
        grid=(b, s // PROJ_TM, dff // FFN_TF),
        in_specs=[
            pl.BlockSpec((1, PROJ_TM, d), row),
            _const_spec((1, d)),
            pl.BlockSpec((1, 1, d), per_batch),
            pl.BlockSpec((1, 1, d), per_batch),
            pl.BlockSpec((1, 1, d), per_batch),
            _const_spec((1, d)),
            pl.BlockSpec((d, FFN_TF), lambda bi, i, j: (0, j)),
            pl.BlockSpec((d, FFN_TF), lambda bi, i, j: (0, j)),
            pl.BlockSpec((FFN_TF, d), lambda bi, i, j: (j, 0)),
        ],
        out_specs=pl.BlockSpec((1, PROJ_TM, d), row),
        out_shape=jax.ShapeDtypeStruct((b, s, d), F32),
        scratch_shapes=[pltpu.VMEM((PROJ_TM, d), BF16), pltpu.VMEM((PROJ_TM, d), F32)],
        compiler_params=_params(("parallel", "parallel", "arbitrary")),
        name="ffn",
    )(x, g.reshape(1, d), sc, sh, gt, g_final.reshape(1, d), w_gate, w_up, w_down)


def kernel(x, c, positions, w_mod, b_mod, g_mix, w_in, ssm_a_re, ssm_a_im, ssm_log_dt,
           ssm_b_re, ssm_b_im, ssm_c_re, ssm_c_im, ssm_d, w_glu, b_glu, g_ssm_out,
           g_attn_out, w_out, g_ffn, w_gate, w_up, w_down, g_final):
    depth = w_mod.shape[0]
    b, s, d = x.shape
    ssm_width = w_glu.shape[1]
    c_pad = jnp.zeros((SUBLANES, d), F32).at[:b].set(c.astype(F32))
    x = x.astype(F32)
    for l in range(depth):
        mod = _modulation(c_pad, w_mod[l], b_mod[l])[:b]
        sh1, sc1, gt1, sh2, sc2, gt2 = [m.reshape(b, 1, d) for m in jnp.split(mod, N_MOD, axis=-1)]
        u, *qkv = _projection(x, g_mix[l], sc1, sh1, w_in[l].astype(BF16), positions, ssm_width)
        tables = _ssm_tables(ssm_a_re[l], ssm_a_im[l], ssm_log_dt[l], ssm_b_re[l], ssm_b_im[l],
                             ssm_c_re[l], ssm_c_im[l])
        y_ssm = _ssm(u, tables, ssm_d[l], w_glu[l].astype(BF16), b_glu[l], g_ssm_out[l])
        y_att, (w_out_b, w_gate_b, w_up_b, w_down_b) = _attention(
            qkv, g_attn_out[l],
            [w_out[l].astype(F32), w_gate[l].astype(F32), w_up[l].astype(F32),
             w_down[l].astype(F32)])
        x = _out_projection(x, y_ssm, y_att, gt1, w_out_b)
        x = _ffn(x, g_ffn[l], sc2, sh2, gt2, g_final, w_gate_b, w_up_b, w_down_b,
                 final_norm=(l == depth - 1))
    return x
```

```python
import functools

import jax
import jax.numpy as jnp
from jax import lax
from jax.experimental import pallas as pl
from jax.experimental.pallas import tpu as pltpu

F32 = jnp.float32
BF16 = jnp.bfloat16

SSM_GROUP = 16
SSM_STATE = 64
HEAD_DIM = 128
DILATION_PAIRS = ((128, 1), (512, 4), (2048, 16))
Q_BLOCK = 128
ROPE_THETA = 10000.0
N_MOD = 6
EPS = 1e-6
NEG = -1e30

LANES = 128
SUBLANES = 8
BF16_ROWS = 16
VMEM_LIMIT = 60 * 1024 * 1024

MOD_TN = 768
PROJ_TM = 512
PROJ_TN = 256
FFN_TF = 512
FFN_SUB = 256
ATT_SKEW_SOFTMAX = 3
ATT_SKEW_PV = 6
SSM_TC = 512
PAIRS = 32
PITCH = PAIRS + 4
PAIR_VREGS = PAIRS // SUBLANES


def _params(sem):
    return pltpu.CompilerParams(dimension_semantics=sem, vmem_limit_bytes=VMEM_LIMIT)


def _const_spec(shape):
    nd = len(shape)
    return pl.BlockSpec(shape, lambda *_: (0,) * nd, pipeline_mode=pl.Buffered(1))


def _mod_kernel(c_ref, w_ref, b_ref, pos_ref, invf_ref, o_ref, cos_ref, sin_ref):
    c = c_ref[...]
    s = (c * jax.nn.sigmoid(c)).astype(BF16)
    o_ref[...] = jnp.dot(s, w_ref[...].astype(BF16), preferred_element_type=F32) + b_ref[...]
    ang = pos_ref[...].astype(F32) * invf_ref[...]
    lane = lax.broadcasted_iota(jnp.int32, (1, HEAD_DIM), 1)
    cos_ref[...] = jnp.cos(ang)
    sin_ref[...] = jnp.sin(ang) * jnp.where(lane < HEAD_DIM // 2, -1.0, 1.0)


def _modulation(c_pad, w_mod, b_mod, positions):
    rows, d = c_pad.shape
    n = w_mod.shape[1]
    steps = n // MOD_TN
    tokens = positions.size
    t_rows = tokens // steps
    assert t_rows * steps == tokens and t_rows % SUBLANES == 0
    half = HEAD_DIM // 2
    inv_freq = ROPE_THETA ** (-jnp.arange(half, dtype=F32) / half)
    inv_freq = jnp.concatenate([inv_freq, inv_freq]).reshape(1, HEAD_DIM)
    table = jax.ShapeDtypeStruct((tokens, HEAD_DIM), F32)
    return pl.pallas_call(
        _mod_kernel,
        grid=(steps,),
        in_specs=[
            pl.BlockSpec((rows, d), lambda j: (0, 0)),
            pl.BlockSpec((d, MOD_TN), lambda j: (0, j)),
            pl.BlockSpec((1, MOD_TN), lambda j: (0, j)),
            pl.BlockSpec((t_rows, 1), lambda j: (j, 0)),
            _const_spec((1, HEAD_DIM)),
        ],
        out_specs=[pl.BlockSpec((rows, MOD_TN), lambda j: (0, j)),
                   pl.BlockSpec((t_rows, HEAD_DIM), lambda j: (j, 0)),
                   pl.BlockSpec((t_rows, HEAD_DIM), lambda j: (j, 0))],
        out_shape=[jax.ShapeDtypeStruct((rows, n), F32), table, table],
        compiler_params=_params(("parallel",)),
        name="mod",
    )(c_pad, w_mod, b_mod.reshape(1, n), positions.reshape(tokens, 1), inv_freq)


def _modulated_norm(x, g, sc, sh):
    ms = jnp.mean(x * x, axis=-1, keepdims=True)
    return x * lax.rsqrt(ms + EPS) * (g * (1.0 + sc)) + sh


def _proj_kernel(x_ref, g_ref, sc_ref, sh_ref, w_ref, cos_ref, sin_ref,
                 u_ref, *rest, ssm_width):
    n_br = len(DILATION_PAIRS)
    qkv_refs, (h_ref, stage) = rest[:n_br], rest[n_br:]
    tm = x_ref.shape[1]
    n_heads = stage.shape[1]
    attn_width = n_heads * HEAD_DIM
    half = HEAD_DIM // 2

    h_ref[...] = _modulated_norm(x_ref[0], g_ref[...], sc_ref[0], sh_ref[0]).astype(BF16)
    def project(first):
        return jnp.dot(h_ref[...], w_ref[:, first:first + PROJ_TN], preferred_element_type=F32)

    for first in range(0, ssm_width, PROJ_TN):
        u_ref[0, :, first:first + PROJ_TN] = project(first).astype(BF16)

    cos = cos_ref[0]
    sin = sin_ref[0]

    for part in range(3):
        for hd0 in range(0, n_heads, PROJ_TN // HEAD_DIM):
            res = project(ssm_width + part * attn_width + hd0 * HEAD_DIM)
            for k in range(PROJ_TN // HEAD_DIM):
                t = res[:, k * HEAD_DIM:(k + 1) * HEAD_DIM]
                if part < 2:
                    t = t * cos + pltpu.roll(t, half, 1) * sin
                if part == 0:
                    t = t * (HEAD_DIM ** -0.5)
                stage[0, hd0 + k] = t
        prev_d = 1
        for bi, (_, d) in enumerate(DILATION_PAIRS):
            ratio = d // prev_d
            assert ratio * prev_d == d and tm % (d * BF16_ROWS) == 0
            seg, prev_seg = tm // d, tm // prev_d
            for hd in range(n_heads):
                cols = slice(part * attn_width + hd * HEAD_DIM,
                             part * attn_width + (hd + 1) * HEAD_DIM)
                if bi == 0:
                    qkv_refs[0][0, :, cols] = stage[0, hd].astype(BF16)
                    continue
                for r in range(d):
                    r_prev, r_sub = r % prev_d, r // prev_d
                    src = stage[bi - 1, hd, pl.ds(r_prev * prev_seg + r_sub, seg, stride=ratio), :]
                    if bi + 1 < n_br:
                        stage[bi, hd, r * seg:(r + 1) * seg, :] = src
                    qkv_refs[bi][0, r, :, cols] = src.astype(BF16)
            prev_d = d


def _projection(x, g, sc, sh, w_in, cos, sin, ssm_width):
    b, s, d = x.shape
    n = w_in.shape[1]
    attn3 = n - ssm_width
    n_heads = attn3 // (3 * HEAD_DIM)
    row = lambda bi, i: (bi, i, 0)
    per_batch = lambda bi, i: (bi, 0, 0)
    out_specs = [pl.BlockSpec((1, PROJ_TM, ssm_width), row)]
    out_shape = [jax.ShapeDtypeStruct((b, s, ssm_width), BF16)]
    for _, dil in DILATION_PAIRS:
        if dil == 1:
            out_specs.append(pl.BlockSpec((1, PROJ_TM, attn3), row))
            out_shape.append(jax.ShapeDtypeStruct((b, s, attn3), BF16))
        else:
            out_specs.append(pl.BlockSpec((1, dil, PROJ_TM // dil, attn3),
                                          lambda bi, i: (bi, 0, i, 0)))
            out_shape.append(jax.ShapeDtypeStruct((b, dil, s // dil, attn3), BF16))
    n_stage = len(DILATION_PAIRS) - 1
    return pl.pallas_call(
        functools.partial(_proj_kernel, ssm_width=ssm_width),
        grid=(b, s // PROJ_TM),
        in_specs=[
            pl.BlockSpec((1, PROJ_TM, d), row),
            _const_spec((1, d)),
            pl.BlockSpec((1, 1, d), per_batch),
            pl.BlockSpec((1, 1, d), per_batch),
            _const_spec((d, n)),
            pl.BlockSpec((1, PROJ_TM, HEAD_DIM), row),
            pl.BlockSpec((1, PROJ_TM, HEAD_DIM), row),
        ],
        out_specs=out_specs,
        out_shape=out_shape,
        scratch_shapes=[pltpu.VMEM((PROJ_TM, d), BF16),
                        pltpu.VMEM((n_stage, n_heads, PROJ_TM, HEAD_DIM), F32)],
        compiler_params=_params(("parallel", "parallel")),
        name="proj",
    )(x, g.reshape(1, d), sc, sh, w_in, cos.reshape(b, s, HEAD_DIM), sin.reshape(b, s, HEAD_DIM))


def _ssm_tables(a_re, a_im, log_dt, b_re, b_im, c_re, c_im):
    a_re = a_re.astype(F32); a_im = a_im.astype(F32)
    b_re = b_re.astype(F32); b_im = b_im.astype(F32)
    c_re = c_re.astype(F32); c_im = c_im.astype(F32)
    dt = jnp.exp(log_dt.astype(F32))[:, None]
    mag = jnp.exp(a_re * dt)
    abar_re = mag * jnp.cos(a_im * dt)
    abar_im = mag * jnp.sin(a_im * dt)
    den = a_re * a_re + a_im * a_im
    nr = abar_re - 1.0
    ni = abar_im
    f_re = (nr * a_re + ni * a_im) / den
    f_im = (ni * a_re - nr * a_im) / den
    bb_re = f_re[..., None] * b_re - f_im[..., None] * b_im
    bb_im = f_re[..., None] * b_im + f_im[..., None] * b_re

    g, p = a_re.shape
    cg = b_re.shape[-1]
    pairs = g // 2
    eye2 = jnp.eye(2, dtype=F32)

    def pair_in(t):
        t = t.transpose(0, 2, 1).reshape(pairs, 2, cg, p)
        return jnp.einsum("rkcp,kj->rkcjp", t, eye2).reshape(pairs, 2 * cg, 2 * p)

    blk = jnp.concatenate([pair_in(bb_re), pair_in(bb_im)], axis=-1)
    per_tile = LANES // (2 * cg)
    sel = jax.nn.one_hot(jnp.arange(pairs) % per_tile, per_tile, dtype=F32)
    bb = jnp.einsum("rcn,rq->rqcn", blk, sel).reshape(pairs, LANES, 4 * p)

    def pair_out(t):
        t = t.transpose(0, 2, 1).reshape(pairs, 2, p, cg)
        return jnp.einsum("rkpc,kj->rkpjc", t, eye2).reshape(pairs, 2 * p, 2 * cg)

    cblk = jnp.concatenate([pair_out(c_re), pair_out(-c_im)], axis=1)
    per_out = (2 * LANES) // (2 * cg)
    osel = jax.nn.one_hot(jnp.arange(pairs) % per_out, per_out, dtype=F32)
    cc = jnp.einsum("rkc,rq->rkqc", cblk, osel).reshape(pairs, 4 * p, 2 * LANES)

    def scan_rows(t):
        return t.reshape(SUBLANES, PAIR_VREGS, LANES).transpose(1, 0, 2)

    return bb.astype(BF16), cc.astype(BF16), scan_rows(abar_re), scan_rows(abar_im)


def _ssm_kernel(u_ref, bb_ref, ar_ref, ai_ref, cc_ref, d_ref, wglu_ref, bglu_ref, g_ref,
                o_ref, zre, zim, st, y_s):
    tc = u_ref.shape[1]

    @pl.when(pl.program_id(1) == 0)
    def _():
        st[...] = jnp.zeros_like(st)

    pairs_per_tile = PAIRS // (u_ref.shape[2] // LANES)
    for r in range(PAIRS):
        j = r // pairs_per_tile
        bu = jnp.dot(u_ref[0, :, j * LANES:(j + 1) * LANES], bb_ref[r],
                     preferred_element_type=F32)
        zre[pl.ds(r, tc, stride=PITCH), :] = bu[:, :LANES]
        zim[pl.ds(r, tc, stride=PITCH), :] = bu[:, LANES:]

    ar = [ar_ref[k] for k in range(PAIR_VREGS)]
    ai = [ai_ref[k] for k in range(PAIR_VREGS)]

    def step(t, carry):
        xr, xi = carry
        base = t * PITCH
        nxr, nxi = [], []
        for k in range(PAIR_VREGS):
            rows = pl.ds(base + k, SUBLANES, stride=PAIR_VREGS)
            nr = ar[k] * xr[k] - ai[k] * xi[k] + zre[rows, :]
            ni = ar[k] * xi[k] + ai[k] * xr[k] + zim[rows, :]
            zre[rows, :] = nr
            zim[rows, :] = ni
            nxr.append(nr)
            nxi.append(ni)
        return tuple(nxr), tuple(nxi)

    carry0 = (tuple(st[0, k] for k in range(PAIR_VREGS)),
              tuple(st[1, k] for k in range(PAIR_VREGS)))
    xr, xi = lax.fori_loop(0, tc, step, carry0, unroll=4)
    for k in range(PAIR_VREGS):
        st[0, k] = xr[k]
        st[1, k] = xi[k]

    out_tile = 2 * LANES
    pairs_per_out = PAIRS // (o_ref.shape[2] // out_tile)
    for j in range(o_ref.shape[2] // out_tile):
        acc = None
        for q in range(pairs_per_out):
            r = j * pairs_per_out + q
            lhs = jnp.concatenate([zre[pl.ds(r, tc, stride=PITCH), :].astype(BF16),
                                   zim[pl.ds(r, tc, stride=PITCH), :].astype(BF16)], axis=1)
            part = jnp.dot(lhs, cc_ref[r], preferred_element_type=F32)
            acc = part if acc is None else acc + part
        cols = slice(j * out_tile, (j + 1) * out_tile)
        y = acc + d_ref[:, cols] * u_ref[0, :, cols].astype(F32)
        y_s[:, cols] = jax.nn.gelu(y)

    vg = y_s[...]
    z = jnp.dot(vg.astype(BF16), wglu_ref[...], preferred_element_type=F32) + bglu_ref[...]
    out = vg * jax.nn.sigmoid(z)
    ms = jnp.mean(out * out, axis=-1, keepdims=True)
    o_ref[0] = (out * lax.rsqrt(ms + EPS) * g_ref[...]).astype(BF16)


def _ssm(proj, tables, d_skip, w_glu, b_glu, g_out):
    b, s, _ = proj.shape
    bb, cc, ar, ai = tables
    w = w_glu.shape[0]
    assert PAIRS * 2 * SSM_GROUP == w
    return pl.pallas_call(
        _ssm_kernel,
        grid=(b, s // SSM_TC),
        in_specs=[
            pl.BlockSpec((1, SSM_TC, w), lambda bi, t: (bi, t, 0)),
            _const_spec(bb.shape),
            _const_spec(ar.shape),
            _const_spec(ai.shape),
            _const_spec(cc.shape),
            _const_spec((1, w)),
            _const_spec((w, w)),
            _const_spec((1, w)),
            _const_spec((1, w)),
        ],
        out_specs=pl.BlockSpec((1, SSM_TC, w), lambda bi, t: (bi, t, 0)),
        out_shape=jax.ShapeDtypeStruct((b, s, w), BF16),
        scratch_shapes=[
            pltpu.VMEM((SSM_TC * PITCH, LANES), F32),
            pltpu.VMEM((SSM_TC * PITCH, LANES), F32),
            pltpu.VMEM((2, PAIR_VREGS, SUBLANES, LANES), F32),
            pltpu.VMEM((SSM_TC, w), F32),
        ],
        compiler_params=_params(("parallel", "arbitrary")),
        name="ssm",
    )(proj, bb, ar, ai, cc, d_skip.reshape(1, w), w_glu, b_glu.reshape(1, w),
      g_out.reshape(1, w))


def _attn_kernel(*refs, n_cast):
    n_br = len(DILATION_PAIRS)
    qkv, refs = refs[:3 * n_br], refs[3 * n_br:]
    qkv = [qkv[3 * bi:3 * bi + 3] for bi in range(n_br)]
    g_ref, refs = refs[0], refs[1:]
    cast_in, refs = refs[:n_cast], refs[n_cast:]
    o_ref, refs = refs[0], refs[1:]
    cast_out, refs = refs[:n_cast], refs[n_cast:]
    o_s, l_s, y_s = refs

    s = o_ref.shape[1]
    n_heads = y_s.shape[0]
    h = pl.program_id(1)

    for src, dst in zip(cast_in, cast_out):
        dst[...] = src[...].astype(BF16)

    def rows(bi, j, r, lo, hi):
        ref = qkv[bi][j]
        return ref[0, lo:hi, :] if len(ref.shape) == 3 else ref[0, r, lo:hi, :]

    def banded_mask(n_keys, steps):
        row = lax.broadcasted_iota(jnp.int32, (Q_BLOCK, n_keys), 0)
        col = lax.broadcasted_iota(jnp.int32, (Q_BLOCK, n_keys), 1)
        dist = row + (n_keys - Q_BLOCK) - col
        return (dist >= 0) & (dist <= steps)

    jobs = []
    for bi, (window, d) in enumerate(DILATION_PAIRS):
        steps = window // d
        span = Q_BLOCK * d
        nb = s // span
        masks = {nk: banded_mask(nk, steps) for nk in (Q_BLOCK, 2 * Q_BLOCK)}
        for r in range(d):
            for n in range(nb):
                lo = (n - 1) * Q_BLOCK if n > 0 else 0
                start = r + n * span
                dst = pl.ds(start, Q_BLOCK) if d == 1 else pl.ds(start, Q_BLOCK, stride=d)
                jobs.append((bi, r, n * Q_BLOCK, lo, (n + 1) * Q_BLOCK, masks, dst))

    def scores(job):
        bi, r, q0, lo, hi, masks, _ = job
        sc = lax.dot_general(rows(bi, 0, r, q0, hi), rows(bi, 1, r, lo, hi),
                             (((1,), (1,)), ((), ())), preferred_element_type=F32)
        return jnp.where(masks[hi - lo], sc, NEG)

    def softmax(sc):
        m = jnp.max(sc, axis=-1, keepdims=True)
        p = jnp.exp(sc - m)
        l = jnp.sum(p, axis=-1, keepdims=True)
        return p.astype(BF16), m, l

    def weighted(job, p, m, l):
        bi, r, _, lo, hi, _, dst = job
        pv = jnp.dot(p, rows(bi, 2, r, lo, hi), preferred_element_type=F32)
        o_s[bi, dst, :] = pv * (1.0 / l)
        l_s[bi, dst, :] = jnp.broadcast_to(m + jnp.log(l), (Q_BLOCK, HEAD_DIM))

    sc_of, sm_of = {}, {}
    for t in range(len(jobs) + ATT_SKEW_PV):
        if t < len(jobs):
            sc_of[t] = scores(jobs[t])
        if 0 <= t - ATT_SKEW_SOFTMAX < len(jobs):
            sm_of[t - ATT_SKEW_SOFTMAX] = softmax(sc_of.pop(t - ATT_SKEW_SOFTMAX))
        if 0 <= t - ATT_SKEW_PV < len(jobs):
            weighted(jobs[t - ATT_SKEW_PV], *sm_of.pop(t - ATT_SKEW_PV))

    lses = [l_s[bi] for bi in range(len(DILATION_PAIRS))]
    top = functools.reduce(jnp.maximum, lses)
    es = [jnp.exp(x - top) for x in lses]
    num = sum(e * o_s[bi] for bi, e in enumerate(es))
    y_s[h] = num / sum(es)

    @pl.when(h == n_heads - 1)
    def _():
        ss = sum(jnp.sum(y_s[i] * y_s[i], axis=-1, keepdims=True) for i in range(n_heads))
        inv = lax.rsqrt(ss / (n_heads * HEAD_DIM) + EPS)
        for i in range(n_heads):
            cols = slice(i * HEAD_DIM, (i + 1) * HEAD_DIM)
            o_ref[0, :, cols] = (y_s[i] * inv * g_ref[:, cols]).astype(BF16)


def _attention(qkv_by_branch, g_out, weights):
    b, s = qkv_by_branch[0].shape[:2]
    w = g_out.shape[0]
    n_heads = w // HEAD_DIM
    steps = b * n_heads

    def head_cols(arr, j):
        if arr.ndim == 3:
            return pl.BlockSpec((1, s, HEAD_DIM), lambda bi, h: (bi, 0, j * n_heads + h))
        return pl.BlockSpec((1,) + arr.shape[1:3] + (HEAD_DIM,),
                            lambda bi, h: (bi, 0, 0, j * n_heads + h))

    qkv_specs = [head_cols(arr, j) for arr in qkv_by_branch for j in range(3)]
    qkv_args = [arr for arr in qkv_by_branch for _ in range(3)]

    def slab(wt):
        rows = wt.shape[0] // steps
        assert rows * steps == wt.shape[0] and rows % BF16_ROWS == 0
        return pl.BlockSpec((rows, wt.shape[1]), lambda bi, h: (bi * n_heads + h, 0))

    n_br = len(DILATION_PAIRS)
    outs = pl.pallas_call(
        functools.partial(_attn_kernel, n_cast=len(weights)),
        grid=(b, n_heads),
        in_specs=qkv_specs + [_const_spec((1, w))] + [slab(wt) for wt in weights],
        out_specs=[pl.BlockSpec((1, s, w), lambda bi, h: (bi, 0, 0), pipeline_mode=pl.Buffered(1))]
        + [slab(wt) for wt in weights],
        out_shape=[jax.ShapeDtypeStruct((b, s, w), BF16)]
        + [jax.ShapeDtypeStruct(wt.shape, BF16) for wt in weights],
        scratch_shapes=[
            pltpu.VMEM((n_br, s, HEAD_DIM), F32),
            pltpu.VMEM((n_br, s, HEAD_DIM), F32),
            pltpu.VMEM((n_heads, s, HEAD_DIM), F32),
        ],
        compiler_params=_params(("parallel", "arbitrary")),
        name="attn",
    )(*qkv_args, g_out.reshape(1, w), *weights)
    return outs[0], outs[1:]


def _outproj_kernel(x_ref, ys_ref, ya_ref, gt_ref, w_ref, o_ref):
    ws = ys_ref.shape[2]
    acc = jnp.dot(ys_ref[0], w_ref[:ws, :], preferred_element_type=F32)
    acc = acc + jnp.dot(ya_ref[0], w_ref[ws:, :], preferred_element_type=F32)
    o_ref[0] = x_ref[0] + gt_ref[0] * acc


def _out_projection(x, y_ssm, y_att, gt, w_out):
    b, s, d = x.shape
    row = lambda bi, i: (bi, i, 0)
    return pl.pallas_call(
        _outproj_kernel,
        grid=(b, s // PROJ_TM),
        in_specs=[
            pl.BlockSpec((1, PROJ_TM, d), row),
            pl.BlockSpec((1, PROJ_TM, y_ssm.shape[2]), row),
            pl.BlockSpec((1, PROJ_TM, y_att.shape[2]), row),
            pl.BlockSpec((1, 1, d), lambda bi, i: (bi, 0, 0)),
            _const_spec(w_out.shape),
        ],
        out_specs=pl.BlockSpec((1, PROJ_TM, d), row),
        out_shape=jax.ShapeDtypeStruct((b, s, d), F32),
        compiler_params=_params(("parallel", "parallel")),
        name="outproj",
    )(x, y_ssm, y_att, gt, w_out)


def _ffn_kernel(x_ref, g_ref, sc_ref, sh_ref, gt_ref, gf_ref, wg_ref, wu_ref, wd_ref,
                o_ref, h_ref, acc_ref, *, final_norm):
    j = pl.program_id(2)

    @pl.when(j == 0)
    def _():
        h_ref[...] = _modulated_norm(x_ref[0], g_ref[...], sc_ref[0], sh_ref[0]).astype(BF16)
        acc_ref[...] = jnp.zeros_like(acc_ref)

    h = h_ref[...]
    acts = []
    for c in range(wg_ref.shape[1] // FFN_SUB):
        cols = slice(c * FFN_SUB, (c + 1) * FFN_SUB)
        gate = jnp.dot(h, wg_ref[:, cols], preferred_element_type=F32)
        up = jnp.dot(h, wu_ref[:, cols], preferred_element_type=F32)
        acts.append((gate * jax.nn.sigmoid(gate) * up).astype(BF16))
    down = None
    for c, act in enumerate(acts):
        part = jnp.dot(act, wd_ref[c * FFN_SUB:(c + 1) * FFN_SUB, :], preferred_element_type=F32)
        down = part if down is None else down + part
    acc_ref[...] += down

    @pl.when(j == pl.num_programs(2) - 1)
    def _():
        y = x_ref[0] + gt_ref[0] * acc_ref[...]
        if final_norm:
            ms = jnp.mean(y * y, axis=-1, keepdims=True)
            y = y * lax.rsqrt(ms + EPS) * gf_ref[...]
        o_ref[0] = y


def _ffn(x, g, sc, sh, gt, g_final, w_gate, w_up, w_down, final_norm):
    b, s, d = x.shape
    dff = w_gate.shape[1]
    row = lambda bi, i, j: (bi, i, 0)
    per_batch = lambda bi, i, j: (bi, 0, 0)
    return pl.pallas_call(
        functools.partial(_ffn_kernel, final_norm=final_norm),
        grid=(b, s // PROJ_TM, dff // FFN_TF),
        in_specs=[
            pl.BlockSpec((1, PROJ_TM, d), row),
            _const_spec((1, d)),
            pl.BlockSpec((1, 1, d), per_batch),
            pl.BlockSpec((1, 1, d), per_batch),
            pl.BlockSpec((1, 1, d), per_batch),
            _const_spec((1, d)),
            pl.BlockSpec((d, FFN_TF), lambda bi, i, j: (0, j)),
            pl.BlockSpec((d, FFN_TF), lambda bi, i, j: (0, j)),
            pl.BlockSpec((FFN_TF, d), lambda bi, i, j: (j, 0)),
        ],
        out_specs=pl.BlockSpec((1, PROJ_TM, d), row),
        out_shape=jax.ShapeDtypeStruct((b, s, d), F32),
        scratch_shapes=[pltpu.VMEM((PROJ_TM, d), BF16), pltpu.VMEM((PROJ_TM, d), F32)],
        compiler_params=_params(("parallel", "parallel", "arbitrary")),
        name="ffn",
    )(x, g.reshape(1, d), sc, sh, gt, g_final.reshape(1, d), w_gate, w_up, w_down)


def kernel(x, c, positions, w_mod, b_mod, g_mix, w_in, ssm_a_re, ssm_a_im, ssm_log_dt,
           ssm_b_re, ssm_b_im, ssm_c_re, ssm_c_im, ssm_d, w_glu, b_glu, g_ssm_out,
           g_attn_out, w_out, g_ffn, w_gate, w_up, w_down, g_final):
    depth = w_mod.shape[0]
    b, s, d = x.shape
    ssm_width = w_glu.shape[1]
    c_pad = jnp.zeros((SUBLANES, d), F32).at[:b].set(c.astype(F32))
    x = x.astype(F32)
    for l in range(depth):
        mod, cos, sin = _modulation(c_pad, w_mod[l], b_mod[l], positions)
        sh1, sc1, gt1, sh2, sc2, gt2 = [m.reshape(b, 1, d)
                                        for m in jnp.split(mod[:b], N_MOD, axis=-1)]
        u, *qkv = _projection(x, g_mix[l], sc1, sh1, w_in[l].astype(BF16), cos, sin, ssm_width)
        tables = _ssm_tables(ssm_a_re[l], ssm_a_im[l], ssm_log_dt[l], ssm_b_re[l], ssm_b_im[l],
                             ssm_c_re[l], ssm_c_im[l])
        y_ssm = _ssm(u, tables, ssm_d[l], w_glu[l].astype(BF16), b_glu[l], g_ssm_out[l])
        y_att, (w_out_b, w_gate_b, w_up_b, w_down_b) = _attention(
            qkv, g_attn_out[l],
            [w_out[l].astype(F32), w_gate[l].astype(F32), w_up[l].astype(F32),
             w_down[l].astype(F32)])
        x = _out_projection(x, y_ssm, y_att, gt1, w_out_b)
        x = _ffn(x, g_ffn[l], sc2, sh2, gt2, g_final, w_gate_b, w_up_b, w_down_b,
                 final_norm=(l == depth - 1))
    return x
```
